```python
import math
import jax, jax.numpy as jnp
from jax import lax
import numpy as np

D_MODEL = 1024
BATCH = 1
SEQ = 16384
DEPTH = 1
DEC_BATCH = 128
DEC_SEQ = 1
PAST_LEN = 8192
PAGE_SIZE = 128

DIFF_HEADS = 4
DIFF_QK = 64
DIFF_V = 2 * DIFF_QK
DSA_HEADS = 4
DSA_HD = 128
IDX_HEADS = 4
IDX_DIM = 64
TOPK_MAX = 256
N_BUCKETS = 32
MAX_EXACT = N_BUCKETS // 2
MAX_DIST = 128
N_ATTN_HEADS = DIFF_HEADS + DSA_HEADS
PEER_HEADS = 8
PEER_NKEYS = 128
PEER_N_EXP = PEER_NKEYS * PEER_NKEYS
PEER_KH = 128
PEER_TOPK = 16
PLE_DIM = 256
QBLK = 128
RMS_EPS = 1e-6
NEG_INF = -1e30

IN_SIZES = (DIFF_HEADS * 2 * DIFF_QK, DIFF_HEADS * 2 * DIFF_QK, DIFF_HEADS * DIFF_V,
            DSA_HEADS * DSA_HD, DSA_HD, DSA_HD,
            IDX_HEADS * IDX_DIM, IDX_DIM, IDX_HEADS, 2 * D_MODEL)
IN_OFFSETS = tuple(sum(IN_SIZES[:i + 1]) for i in range(len(IN_SIZES) - 1))
N_IN = sum(IN_SIZES)
A_WIDTH = DIFF_HEADS * DIFF_V
B_WIDTH = DSA_HEADS * DSA_HD

kernel_name = "hybrid_diffattn_dsa_peer_decode_step"


def rms_norm(x, g):
    xf = x.astype(jnp.float32)
    y = xf * lax.rsqrt(jnp.mean(xf * xf, axis=-1, keepdims=True) + RMS_EPS)
    return (y * g.astype(jnp.float32)).astype(x.dtype)


def rel_bucket(dist):
    n = jnp.maximum(dist, 0)
    nf = jnp.maximum(n, 1).astype(jnp.float32)
    large = MAX_EXACT + (jnp.log(nf / MAX_EXACT) / math.log(MAX_DIST / MAX_EXACT)
                         * (N_BUCKETS - MAX_EXACT)).astype(jnp.int32)
    return jnp.where(n < MAX_EXACT, n, jnp.minimum(large, N_BUCKETS - 1))


def mixer_inputs(h, w_in, dqn, dkn, sqn, skn):
    b, t, _ = h.shape
    z = h @ w_in
    zdq, zdk, zdv, zsq, zsk, zsv, ziq, zik, ziw, zg = jnp.split(z, IN_OFFSETS, axis=-1)
    qd = rms_norm(zdq.reshape(b, t, DIFF_HEADS, 2, DIFF_QK), dqn)
    kd = rms_norm(zdk.reshape(b, t, DIFF_HEADS, 2, DIFF_QK), dkn)
    vd = zdv.reshape(b, t, DIFF_HEADS, DIFF_V)
    qs = rms_norm(zsq.reshape(b, t, DSA_HEADS, DSA_HD), sqn)
    ks = rms_norm(zsk, skn)
    qi = ziq.reshape(b, t, IDX_HEADS, IDX_DIM)
    ga, gb = jnp.split(zg, 2, axis=-1)
    return qd, kd, vd, qs, ks, zsv, qi, zik, ziw, ga, gb


def diff_lambda(lam_p, lam_init):
    lp = lam_p.astype(jnp.float32)
    return jnp.exp(jnp.sum(lp[0] * lp[1])) - jnp.exp(jnp.sum(lp[2] * lp[3])) + lam_init


def diff_attention(qd, kd, vd, q_pos, k_pos, rel_tab, lam, lam_init, subln_g):
    b, tq = qd.shape[0], qd.shape[1]
    dist = q_pos[:, None] - k_pos[None, :]
    bias = jnp.moveaxis(rel_tab[rel_bucket(dist)].astype(jnp.float32), -1, 0)
    logits = jnp.einsum('bqhcd,bkhcd->cbhqk', qd, kd).astype(jnp.float32) * (DIFF_QK ** -0.5) + bias
    logits = jnp.where(dist >= 0, logits, NEG_INF)
    probs = jax.nn.softmax(logits, axis=-1)
    attn = probs[0] - lam * probs[1]
    out = jnp.einsum('bhqk,bkhd->bqhd', attn.astype(vd.dtype), vd)
    out = rms_norm(out, subln_g) * (1.0 - lam_init)
    return out.reshape(b, tq, A_WIDTH)


def dsa_select(qi, wi, ki, q_pos, k_pos, n_sel):
    s = jax.nn.relu(jnp.einsum('bqhd,bkd->bqhk', qi, ki))
    score = jnp.einsum('bqh,bqhk->bqk', wi, s).astype(jnp.float32)
    score = jnp.where(k_pos[None, :] <= q_pos[:, None], score, NEG_INF)
    _, sel = lax.top_k(score, n_sel)
    return sel


def dsa_attention(qs, k_sel, v_sel, q_pos, sel, rel_tab):
    b, tq = qs.shape[0], qs.shape[1]
    dist = q_pos[None, :, None] - sel
    bias = jnp.moveaxis(rel_tab[rel_bucket(dist)].astype(jnp.float32), -1, 1)
    logits = jnp.einsum('bqhd,bqkd->bhqk', qs, k_sel).astype(jnp.float32) * (DSA_HD ** -0.5) + bias
    logits = jnp.where((dist >= 0)[:, None], logits, NEG_INF)
    p = jax.nn.softmax(logits, axis=-1)
    out = jnp.einsum('bhqk,bqkd->bqhd', p.astype(v_sel.dtype), v_sel)
    return out.reshape(b, tq, B_WIDTH)


def gather_pages(pool, page_table):
    g = pool[page_table]
    return g.reshape(g.shape[0], g.shape[1] * g.shape[2], *g.shape[3:])


def paged_select(pool, page_table, new_rows, sel):
    db = sel.shape[0]
    sp = jnp.minimum(sel, PAST_LEN - 1)
    phys = jnp.take_along_axis(page_table, (sp // PAGE_SIZE).reshape(db, -1), axis=1).reshape(sel.shape)
    past_rows = pool[phys, sp % PAGE_SIZE]
    bidx = jnp.arange(db)[:, None, None]
    fresh = new_rows[bidx, jnp.clip(sel - PAST_LEN, 0, new_rows.shape[1] - 1)]
    return jnp.where((sel < PAST_LEN)[..., None], past_rows, fresh)


def peer(h, wq, sub_keys, eu, ev):
    t = h.shape[0]
    q = (h @ wq).reshape(t, PEER_HEADS, 2, PEER_KH)
    s1 = jnp.einsum('thd,hnd->thn', q[:, :, 0], sub_keys[0]).astype(jnp.float32)
    s2 = jnp.einsum('thd,hnd->thn', q[:, :, 1], sub_keys[1]).astype(jnp.float32)
    v1, i1 = lax.top_k(s1, PEER_TOPK)
    v2, i2 = lax.top_k(s2, PEER_TOPK)
    cs = (v1[..., :, None] + v2[..., None, :]).reshape(t, PEER_HEADS, PEER_TOPK * PEER_TOPK)
    ci = (i1[..., :, None] * PEER_NKEYS + i2[..., None, :]).reshape(t, PEER_HEADS, PEER_TOPK * PEER_TOPK)
    top_s, top_p = lax.top_k(cs, PEER_TOPK)
    idx = jnp.take_along_axis(ci, top_p, axis=-1)
    g = jax.nn.softmax(top_s, axis=-1)
    act = jax.nn.gelu(jnp.einsum('td,thkd->thk', h, eu[idx]), approximate=False)
    return jnp.einsum('thk,thkd->td', (g * act).astype(h.dtype), ev[idx])


def layer_tail(x, da, sa, ga, gb, p, wa, wb, wo, n2, pwq, pkeys, pu, pv, pg, pp, blocked):
    merged = jax.nn.sigmoid(ga) * (da @ wa) + jax.nn.sigmoid(gb) * (sa @ wb)
    x1 = x + merged @ wo
    h2 = rms_norm(x1, n2).reshape(-1, D_MODEL)
    if blocked:
        ff = lax.map(lambda hb: peer(hb, pwq, pkeys, pu, pv), h2.reshape(-1, QBLK, D_MODEL))
    else:
        ff = peer(h2, pwq, pkeys, pu, pv)
    x2 = x1 + ff.reshape(x.shape)
    return x2 + jax.nn.sigmoid(x2 @ pg) * (p @ pp)


def to_blocks(a):
    b, s = a.shape[0], a.shape[1]
    return jnp.moveaxis(a.reshape(b, s // QBLK, QBLK, *a.shape[2:]), 1, 0)


def from_blocks(a):
    a = jnp.moveaxis(a, 0, 1)
    return a.reshape(a.shape[0], a.shape[1] * a.shape[2], *a.shape[3:])


def setup_inputs(seed: int = 0) -> dict:
    key = jax.random.key(seed)
    ks = jax.random.split(key, 32)
    f32 = jnp.float32

    def nrm(k, shape, scale):
        return jax.random.normal(k, shape, f32) * scale

    def gain(k, shape):
        return 1.0 + 0.05 * jax.random.normal(k, shape, f32)

    n_pages = PAST_LEN // PAGE_SIZE
    n_used = DEC_BATCH * n_pages
    n_phys = n_used + max(1, n_used // 4)
    page_table = jax.random.permutation(ks[0], n_phys)[:n_used].reshape(DEC_BATCH, n_pages).astype(jnp.int32)
    return {
        "x_prompt": nrm(ks[1], (BATCH, SEQ, D_MODEL), 1.0),
        "x_sample": nrm(ks[2], (DEC_BATCH, DEC_SEQ, D_MODEL), 1.0),
        "p_prompt": nrm(ks[3], (DEPTH, BATCH, SEQ, PLE_DIM), 1.0),
        "p_sample": nrm(ks[4], (DEPTH, DEC_BATCH, DEC_SEQ, PLE_DIM), 1.0),
        "cache_diff_k": nrm(ks[5], (DEPTH, n_phys, PAGE_SIZE, DIFF_HEADS, 2, DIFF_QK), 1.0),
        "cache_diff_v": nrm(ks[6], (DEPTH, n_phys, PAGE_SIZE, DIFF_HEADS, DIFF_V), 1.0),
        "cache_dsa_k": nrm(ks[7], (DEPTH, n_phys, PAGE_SIZE, DSA_HD), 1.0),
        "cache_dsa_v": nrm(ks[8], (DEPTH, n_phys, PAGE_SIZE, DSA_HD), 1.0),
        "cache_idx_k": nrm(ks[9], (DEPTH, n_phys, PAGE_SIZE, IDX_DIM), 1.0),
        "page_table": page_table,
        "rel_bias": nrm(ks[10], (N_BUCKETS, N_ATTN_HEADS), 0.5),
        "norm1_g": gain(ks[11], (DEPTH, D_MODEL)),
        "w_in": nrm(ks[12], (DEPTH, D_MODEL, N_IN), D_MODEL ** -0.5),
        "diff_q_norm": gain(ks[13], (DEPTH, DIFF_QK)),
        "diff_k_norm": gain(ks[14], (DEPTH, DIFF_QK)),
        "diff_lambda_p": nrm(ks[15], (DEPTH, 4, DIFF_QK), 0.1),
        "diff_subln": gain(ks[16], (DEPTH, DIFF_V)),
        "dsa_q_norm": gain(ks[17], (DEPTH, DSA_HD)),
        "dsa_k_norm": gain(ks[18], (DEPTH, DSA_HD)),
        "w_branch_a": nrm(ks[19], (DEPTH, A_WIDTH, D_MODEL), A_WIDTH ** -0.5),
        "w_branch_b": nrm(ks[20], (DEPTH, B_WIDTH, D_MODEL), B_WIDTH ** -0.5),
        "w_out": nrm(ks[21], (DEPTH, D_MODEL, D_MODEL), D_MODEL ** -0.5),
        "norm2_g": gain(ks[22], (DEPTH, D_MODEL)),
        "peer_wq": nrm(ks[23], (DEPTH, D_MODEL, PEER_HEADS * 2 * PEER_KH), D_MODEL ** -0.5),
        "peer_keys": nrm(ks[24], (DEPTH, 2, PEER_HEADS, PEER_NKEYS, PEER_KH), PEER_KH ** -0.5),
        "peer_u": nrm(ks[25], (DEPTH, PEER_N_EXP, D_MODEL), D_MODEL ** -0.5),
        "peer_v": nrm(ks[26], (DEPTH, PEER_N_EXP, D_MODEL), PEER_HEADS ** -0.5),
        "ple_gate": nrm(ks[27], (DEPTH, D_MODEL, D_MODEL), D_MODEL ** -0.5),
        "ple_proj": nrm(ks[28], (DEPTH, PLE_DIM, D_MODEL), PLE_DIM ** -0.5),
    }


def reference(x_prompt, x_sample, p_prompt, p_sample, cache_diff_k, cache_diff_v, cache_dsa_k, cache_dsa_v,
              cache_idx_k, page_table, rel_bias, norm1_g, w_in, diff_q_norm, diff_k_norm, diff_lambda_p,
              diff_subln, dsa_q_norm, dsa_k_norm, w_branch_a, w_branch_b, w_out, norm2_g, peer_wq, peer_keys,
              peer_u, peer_v, ple_gate, ple_proj):
    n_sel_p = min(TOPK_MAX, SEQ // 4)
    n_sel_s = min(TOPK_MAX, (PAST_LEN + DEC_SEQ) // 4)
    k_pos_p = jnp.arange(SEQ, dtype=jnp.int32)
    q_pos_s = PAST_LEN + jnp.arange(DEC_SEQ, dtype=jnp.int32)
    k_pos_s = jnp.arange(PAST_LEN + DEC_SEQ, dtype=jnp.int32)
    tab_d = rel_bias[:, :DIFF_HEADS]
    tab_s = rel_bias[:, DIFF_HEADS:]
    xp, xs = x_prompt, x_sample
    dk_p, dv_p, sk_p, sv_p, ik_p = [], [], [], [], []
    dk_s, dv_s, sk_s, sv_s, ik_s = [], [], [], [], []
    for l in range(DEPTH):
        lam_init = 0.8 - 0.6 * math.exp(-0.3 * l)
        lam = diff_lambda(diff_lambda_p[l], lam_init)
        hp = rms_norm(xp, norm1_g[l])
        qd, kd, vd, qs, ksp, vsp, qi, ki, wi, ga, gb = mixer_inputs(
            hp, w_in[l], diff_q_norm[l], diff_k_norm[l], dsa_q_norm[l], dsa_k_norm[l])
        bidx_p = jnp.arange(BATCH)[:, None, None]

        def prompt_block(args):
            qd_b, qs_b, qi_b, wi_b, start = args
            q_pos = start + jnp.arange(QBLK, dtype=jnp.int32)
            da_b = diff_attention(qd_b, kd, vd, q_pos, k_pos_p, tab_d, lam, lam_init, diff_subln[l])
            sel = dsa_select(qi_b, wi_b, ki, q_pos, k_pos_p, n_sel_p)
            sa_b = dsa_attention(qs_b, ksp[bidx_p, sel], vsp[bidx_p, sel], q_pos, sel, tab_s)
            return da_b, sa_b

        starts = jnp.arange(SEQ // QBLK, dtype=jnp.int32) * QBLK
        da_blk, sa_blk = lax.map(prompt_block, (to_blocks(qd), to_blocks(qs), to_blocks(qi), to_blocks(wi), starts))
        xp = layer_tail(xp, from_blocks(da_blk), from_blocks(sa_blk), ga, gb, p_prompt[l], w_branch_a[l],
                        w_branch_b[l], w_out[l], norm2_g[l], peer_wq[l], peer_keys[l], peer_u[l], peer_v[l],
                        ple_gate[l], ple_proj[l], True)
        dk_p.append(kd); dv_p.append(vd); sk_p.append(ksp); sv_p.append(vsp); ik_p.append(ki)
        hs = rms_norm(xs, norm1_g[l])
        qd2, kd2, vd2, qs2, ks2, vs2, qi2, ki2, wi2, ga2, gb2 = mixer_inputs(
            hs, w_in[l], diff_q_norm[l], diff_k_norm[l], dsa_q_norm[l], dsa_k_norm[l])
        kd_all = jnp.concatenate([gather_pages(cache_diff_k[l], page_table), kd2], axis=1)
        vd_all = jnp.concatenate([gather_pages(cache_diff_v[l], page_table), vd2], axis=1)
        da_s = diff_attention(qd2, kd_all, vd_all, q_pos_s, k_pos_s, tab_d, lam, lam_init, diff_subln[l])
        ki_all = jnp.concatenate([gather_pages(cache_idx_k[l], page_table), ki2], axis=1)
        sel_s = dsa_select(qi2, wi2, ki_all, q_pos_s, k_pos_s, n_sel_s)
        ks_sel = paged_select(cache_dsa_k[l], page_table, ks2, sel_s)
        vs_sel = paged_select(cache_dsa_v[l], page_table, vs2, sel_s)
        sa_s = dsa_attention(qs2, ks_sel, vs_sel, q_pos_s, sel_s, tab_s)
        xs = layer_tail(xs, da_s, sa_s, ga2, gb2, p_sample[l], w_branch_a[l], w_branch_b[l], w_out[l],
                        norm2_g[l], peer_wq[l], peer_keys[l], peer_u[l], peer_v[l], ple_gate[l], ple_proj[l], False)
        dk_s.append(kd2); dv_s.append(vd2); sk_s.append(ks2); sv_s.append(vs2); ik_s.append(ki2)
    y_prompt, y_sample = xp, xs
    new_diff_k_p = jnp.stack(dk_p); new_diff_v_p = jnp.stack(dv_p)
    new_dsa_k_p = jnp.stack(sk_p); new_dsa_v_p = jnp.stack(sv_p); new_idx_k_p = jnp.stack(ik_p)
    new_diff_k_s = jnp.stack(dk_s); new_diff_v_s = jnp.stack(dv_s)
    new_dsa_k_s = jnp.stack(sk_s); new_dsa_v_s = jnp.stack(sv_s); new_idx_k_s = jnp.stack(ik_s)
    return (y_prompt, y_sample, new_diff_k_p, new_diff_v_p, new_dsa_k_p, new_dsa_v_p, new_idx_k_p,
            new_diff_k_s, new_diff_v_s, new_dsa_k_s, new_dsa_v_s, new_idx_k_s)
```

```python
import functools
import math

import numpy as np
import jax
import jax.numpy as jnp
from jax import lax
from jax.experimental import pallas as pl
from jax.experimental.pallas import tpu as pltpu

F32 = jnp.float32
BF16 = jnp.bfloat16
I32 = jnp.int32

DIFF_HEADS = 4
DIFF_QK = 64
DIFF_V = 128
DSA_HEADS = 4
DSA_HD = 128
IDX_HEADS = 4
IDX_DIM = 64
TOPK_MAX = 256
N_BUCKETS = 32
MAX_EXACT = 16
MAX_DIST = 128
PEER_HEADS = 8
PEER_NKEYS = 128
PEER_KH = 128
PEER_TOPK = 16
RMS_EPS = 1e-6
NEG_INF = -1e30

LANES = 128
VMEM_LIMIT = 56 * 1024 * 1024

A_WIDTH = DIFF_HEADS * DIFF_V
B_WIDTH = DSA_HEADS * DSA_HD
DQ_W = DIFF_HEADS * 2 * DIFF_QK
IQ_W = IDX_HEADS * IDX_DIM

OFF_DQ = 0
OFF_DK = OFF_DQ + DQ_W
OFF_DV = OFF_DK + DQ_W
OFF_SQ = OFF_DV + A_WIDTH
OFF_SK = OFF_SQ + B_WIDTH
OFF_SV = OFF_SK + DSA_HD
OFF_IQ = OFF_SV + DSA_HD
OFF_IKW = OFF_IQ + IQ_W
OFF_G = OFF_IKW + LANES
IKW_USED = IDX_DIM + IDX_HEADS


def _bucket_table():
    d = np.arange(MAX_DIST)
    nf = np.maximum(d, 1).astype(np.float64)
    val = np.log(nf / MAX_EXACT) / math.log(MAX_DIST / MAX_EXACT) * (N_BUCKETS - MAX_EXACT)
    frac = np.abs(val - np.round(val))
    assert np.all(frac[MAX_EXACT + 1:] > 1e-4)
    large = MAX_EXACT + np.floor(val + 1e-9).astype(np.int64)
    return np.where(d < MAX_EXACT, d, np.minimum(large, N_BUCKETS - 1)).astype(np.int32)


_BUCKET = _bucket_table()


def _bucket_of(dist):
    dist = np.asarray(dist)
    return np.where(dist < MAX_DIST, _BUCKET[np.clip(dist, 0, MAX_DIST - 1)], N_BUCKETS - 1)


def _cparams(sem):
    return pltpu.CompilerParams(dimension_semantics=sem, vmem_limit_bytes=VMEM_LIMIT)


def _rms(x, gain):
    return x * lax.rsqrt(jnp.mean(x * x, axis=-1, keepdims=True) + RMS_EPS) * gain


def _proj_kernel(x_ref, g1_ref, w_ref, dqn_ref, dkn_ref, sqn_ref, skn_ref, g64_ref, g128_ref,
                 qd_ref, kd_ref, kdb_ref, vd_ref, vdb_ref, qs_ref, ks_ref, ksb_ref,
                 vs_ref, vsb_ref, qi_ref, ikw_ref, kib_ref, gate_ref):
    hb = _rms(x_ref[...], g1_ref[...]).astype(BF16)

    def seg(off, n):
        return jnp.dot(hb, w_ref[:, off:off + n], preferred_element_type=F32)

    def group_norm(z, gmat, gain):
        sq = z * z
        hi = sq.astype(BF16)
        lo = (sq - hi.astype(F32)).astype(BF16)
        ms = (jnp.dot(hi, gmat, preferred_element_type=F32)
              + jnp.dot(lo, gmat, preferred_element_type=F32))
        return z * lax.rsqrt(ms + RMS_EPS) * gain

    g64 = g64_ref[...]
    g128 = g128_ref[...]
    qd = group_norm(seg(OFF_DQ, DQ_W), g64, dqn_ref[...])
    qd_ref[...] = (qd * (DIFF_QK ** -0.5)).astype(BF16)
    kd = group_norm(seg(OFF_DK, DQ_W), g64, dkn_ref[...])
    kd_ref[...] = kd
    kdb_ref[...] = kd.astype(BF16)
    vd = seg(OFF_DV, A_WIDTH)
    vd_ref[...] = vd
    vdb_ref[...] = vd.astype(BF16)
    qs = group_norm(seg(OFF_SQ, B_WIDTH), g128, sqn_ref[...])
    qs_ref[...] = (qs * (DSA_HD ** -0.5)).astype(BF16)
    ks = _rms(seg(OFF_SK, DSA_HD), skn_ref[...])
    ks_ref[...] = ks
    ksb_ref[...] = ks.astype(BF16)
    vs = seg(OFF_SV, DSA_HD)
    vs_ref[...] = vs
    vsb_ref[...] = vs.astype(BF16)
    qi_ref[...] = seg(OFF_IQ, IQ_W).astype(BF16)
    ikw = seg(OFF_IKW, LANES)
    ikw_ref[...] = ikw
    kib_ref[...] = ikw.astype(BF16)
    d_model = x_ref.shape[1]
    for half in range(2):
        zg = seg(OFF_G + half * d_model, d_model)
        gate_ref[:, half * d_model:(half + 1) * d_model] = jax.nn.sigmoid(zg).astype(BF16)


def _proj(x, g1, w_pad, dqn, dkn, sqn, skn, g64, g128):
    t, d = x.shape
    tb = min(256, t)
    n_in = w_pad.shape[1]
    row = lambda n: pl.BlockSpec((tb, n), lambda i: (i, 0))
    full = lambda a: pl.BlockSpec(a.shape, lambda i: (0,) * a.ndim)
    outs = [
        (DQ_W, BF16),
        (DQ_W, F32), (DQ_W, BF16),
        (A_WIDTH, F32), (A_WIDTH, BF16),
        (B_WIDTH, BF16),
        (DSA_HD, F32), (DSA_HD, BF16),
        (DSA_HD, F32), (DSA_HD, BF16),
        (IQ_W, BF16),
        (LANES, F32), (LANES, BF16),
        (2 * d, BF16),
    ]
    return pl.pallas_call(
        _proj_kernel,
        grid=(t // tb,),
        in_specs=[row(d), full(g1), full(w_pad), full(dqn), full(dkn), full(sqn), full(skn),
                  full(g64), full(g128)],
        out_specs=[row(n) for n, _ in outs],
        out_shape=[jax.ShapeDtypeStruct((t, n), dt) for n, dt in outs],
        compiler_params=_cparams(("parallel",)),
        name="proj",
    )(x, g1, w_pad, dqn, dkn, sqn, skn, g64, g128)


def _diff_lambda(lp, lam_init):
    e1 = jnp.exp(jnp.sum(lp[0:1] * lp[1:2], axis=-1, keepdims=True))
    e2 = jnp.exp(jnp.sum(lp[2:3] * lp[3:4], axis=-1, keepdims=True))
    return e1 - e2 + lam_init


def _diff_attn_kernel(q_ref, k_ref, v_ref, bd_ref, bp_ref, lamp_ref, sub_ref, o_ref,
                      m_sc, l_sc, acc_sc, *, lam_init):
    qi = pl.program_id(1)
    ki = pl.program_id(2)
    tq = q_ref.shape[2]
    tk = k_ref.shape[2]

    @pl.when(ki == 0)
    def _():
        m_sc[...] = jnp.full(m_sc.shape, NEG_INF, F32)
        l_sc[...] = jnp.zeros(l_sc.shape, F32)
        acc_sc[...] = jnp.zeros(acc_sc.shape, F32)

    def step(kind):
        v = v_ref[0]
        for c in range(2):
            s = lax.dot_general(q_ref[0, c], k_ref[0, c], (((1,), (1,)), ((), ())),
                                preferred_element_type=F32)
            if kind == 1:
                s = s + bp_ref[0]
            if kind == 2:
                r = lax.broadcasted_iota(I32, (tq, tk), 0)
                cc = lax.broadcasted_iota(I32, (tq, tk), 1)
                s = jnp.where(r >= cc, s + bd_ref[0], NEG_INF)
            m_prev = m_sc[c]
            m_new = jnp.maximum(m_prev, jnp.max(s, axis=-1, keepdims=True))
            alpha = jnp.exp(m_prev - m_new)
            p = jnp.exp(s - m_new)
            l_sc[c] = alpha * l_sc[c] + jnp.sum(p, axis=-1, keepdims=True)
            acc_sc[c] = alpha * acc_sc[c] + jnp.dot(p.astype(BF16), v, preferred_element_type=F32)
            m_sc[c] = m_new

    pl.when(ki < qi - 1)(lambda: step(0))
    pl.when(ki == qi - 1)(lambda: step(1))

    @pl.when(ki == qi)
    def _():
        step(2)
        lam = _diff_lambda(lamp_ref[...], lam_init)
        o = acc_sc[0] / l_sc[0] - lam * (acc_sc[1] / l_sc[1])
        o_ref[...] = (_rms(o, sub_ref[...]) * (1.0 - lam_init)).astype(o_ref.dtype)


def _diff_attn(q4, k4, v3, bias_d, bias_p, lam_p, subln, lam_init, blk):
    h, _, t, dqk = q4.shape
    dv = v3.shape[2]
    nb = t // blk
    kern = functools.partial(_diff_attn_kernel, lam_init=lam_init)
    return pl.pallas_call(
        kern,
        grid=(h, nb, nb),
        in_specs=[
            pl.BlockSpec((1, 2, blk, dqk), lambda hh, qi, ki: (hh, 0, qi, 0)),
            pl.BlockSpec((1, 2, blk, dqk), lambda hh, qi, ki: (hh, 0, jnp.minimum(ki, qi), 0)),
            pl.BlockSpec((1, blk, dv), lambda hh, qi, ki: (hh, jnp.minimum(ki, qi), 0)),
            pl.BlockSpec((1, blk, blk), lambda hh, qi, ki: (hh, 0, 0)),
            pl.BlockSpec((1, blk, blk), lambda hh, qi, ki: (hh, 0, 0)),
            pl.BlockSpec(lam_p.shape, lambda hh, qi, ki: (0, 0)),
            pl.BlockSpec(subln.shape, lambda hh, qi, ki: (0, 0)),
        ],
        out_specs=pl.BlockSpec((blk, dv), lambda hh, qi, ki: (qi, hh)),
        out_shape=jax.ShapeDtypeStruct((t, h * dv), BF16),
        scratch_shapes=[pltpu.VMEM((2, blk, 1), F32), pltpu.VMEM((2, blk, 1), F32),
                        pltpu.VMEM((2, blk, dv), F32)],
        compiler_params=_cparams(("parallel", "parallel", "arbitrary")),
        name="diff_attn",
    )(q4, k4, v3, bias_d, bias_p, lam_p, subln)


def _sort_key(s):
    s = jnp.where(s == 0.0, 0.0, s)
    bits = pltpu.bitcast(s, I32)
    return jnp.where(bits < 0, bits ^ jnp.int32(0x7FFFFFFF), bits)


def _count_tiles(keys_sc, n_tiles, pred, rows):
    def body(j, acc):
        off = pl.multiple_of(j * LANES, LANES)
        return acc + jnp.where(pred(keys_sc[:, pl.ds(off, LANES)]), 1.0, 0.0)

    acc = lax.fori_loop(0, n_tiles, body, jnp.zeros((rows, LANES), F32))
    return jnp.sum(acc, axis=-1, keepdims=True)


def _kth_largest_key(keys_sc, n_tiles, k_sel, rows):
    def bis(i, t):
        cand = t ^ jnp.left_shift(jnp.int32(1), 31 - i)
        cand_b = jnp.broadcast_to(cand, (rows, LANES))
        cnt = _count_tiles(keys_sc, n_tiles, lambda kt: kt >= cand_b, rows)
        return jnp.where(cnt >= float(k_sel), cand, t)

    t0 = jnp.full((rows, 1), -2 ** 31, I32)
    return lax.fori_loop(0, 32, bis, t0)


def _tie_constants():
    r = lax.broadcasted_iota(I32, (LANES, LANES), 0)
    c = lax.broadcasted_iota(I32, (LANES, LANES), 1)
    tri = jnp.where(r < c, 1.0, 0.0).astype(BF16)
    ones = jnp.ones((LANES, LANES), BF16)
    return tri, ones


def _select_tile(kt, t_b, m_b, carry, tri, ones):
    eq = kt == t_b
    eqb = jnp.where(eq, 1.0, 0.0).astype(BF16)
    rank = carry + jnp.dot(eqb, tri, preferred_element_type=F32)
    sel = jnp.where(kt > t_b, 1.0, jnp.where(eq, jnp.where(rank < m_b, 1.0, 0.0), 0.0))
    return sel, carry + jnp.dot(eqb, ones, preferred_element_type=F32)


def _dsa_prompt_kernel(qi_ref, ikw_ref, qs_ref, kit_ref, kst_ref, vs_ref, bd_ref, bp_ref,
                       o_ref, keys_sc, msk_sc, m_sc, l_sc, acc_sc, *, k_sel, ch):
    b = pl.program_id(0)
    rows = o_ref.shape[0]
    sub = ch // LANES
    n_tiles = b + 1
    n_ch = (n_tiles + sub - 1) // sub
    n_tiles_pad = n_ch * sub
    row_pos = b * rows + lax.broadcasted_iota(I32, (rows, 1), 0)

    w = ikw_ref[:, IDX_DIM:IDX_DIM + IDX_HEADS]
    wb = [jnp.broadcast_to(w[:, h:h + 1], (rows, ch)) for h in range(IDX_HEADS)]

    def score_chunk(c, _):
        off = pl.multiple_of(c * ch, ch)
        kt = kit_ref[:, pl.ds(off, ch)]
        s = jnp.zeros((rows, ch), F32)
        for h in range(IDX_HEADS):
            d = jnp.dot(qi_ref[h], kt, preferred_element_type=F32)
            s = s + wb[h] * jnp.maximum(d, 0.0)
        col_pos = off + lax.broadcasted_iota(I32, (1, ch), 1)
        s = jnp.where(col_pos <= row_pos, s, NEG_INF)
        keys_sc[:, pl.ds(off, ch)] = _sort_key(s)
        return 0

    lax.fori_loop(0, n_ch, score_chunk, 0)

    t = _kth_largest_key(keys_sc, n_tiles_pad, k_sel, rows)
    t_b = jnp.broadcast_to(t, (rows, LANES))
    c_gt = _count_tiles(keys_sc, n_tiles_pad, lambda kt: kt > t_b, rows)
    m_b = jnp.broadcast_to(float(k_sel) - c_gt, (rows, LANES))

    m_sc[...] = jnp.full(m_sc.shape, NEG_INF, F32)
    l_sc[...] = jnp.zeros(l_sc.shape, F32)
    acc_sc[...] = jnp.zeros(acc_sc.shape, F32)
    tri, ones = _tie_constants()

    def attend_chunk(c, carry, near):
        off = pl.multiple_of(c * ch, ch)
        for j in range(sub):
            kt = keys_sc[:, pl.ds(off + j * LANES, LANES)]
            sel, carry = _select_tile(kt, t_b, m_b, carry, tri, ones)
            if near:
                col_pos = off + j * LANES + lax.broadcasted_iota(I32, (1, LANES), 1)
                sel = jnp.where(col_pos <= row_pos, sel, 0.0)
            msk_sc[:, j * LANES:(j + 1) * LANES] = jnp.where(sel > 0.5, 0.0, NEG_INF)
        kst = kst_ref[:, pl.ds(off, ch)]
        vch = vs_ref[pl.ds(off, ch), :]
        for h in range(DSA_HEADS):
            s = jnp.dot(qs_ref[:, h * DSA_HD:(h + 1) * DSA_HD], kst, preferred_element_type=F32)
            if near:
                parts = []
                for j in range(sub):
                    jt = c * sub + j
                    bias = jnp.where(jt == b, bd_ref[h], jnp.where(jt == b - 1, bp_ref[h], 0.0))
                    parts.append(s[:, j * LANES:(j + 1) * LANES] + bias)
                s = jnp.concatenate(parts, axis=1)
            s = s + msk_sc[...]
            m_prev = m_sc[h]
            m_new = jnp.maximum(m_prev, jnp.max(s, axis=-1, keepdims=True))
            alpha = jnp.exp(m_prev - m_new)
            p = jnp.exp(s - m_new)
            l_sc[h] = alpha * l_sc[h] + jnp.sum(p, axis=-1, keepdims=True)
            acc_sc[h] = alpha * acc_sc[h] + jnp.dot(p.astype(BF16), vch, preferred_element_type=F32)
            m_sc[h] = m_new
        return carry

    c_near = (jnp.maximum(b - 1, 0) * LANES) // ch
    carry = lax.fori_loop(0, c_near, lambda c, cr: attend_chunk(c, cr, False),
                          jnp.zeros((rows, LANES), F32))
    lax.fori_loop(c_near, n_ch, lambda c, cr: attend_chunk(c, cr, True), carry)
    for h in range(DSA_HEADS):
        o_ref[:, h * DSA_HD:(h + 1) * DSA_HD] = (acc_sc[h] / l_sc[h]).astype(o_ref.dtype)


def _dsa_prompt(qi4, ikw, qs, kit, kst, vsb, bias_d, bias_p, k_sel, ch):
    t = qs.shape[0]
    rows = LANES
    full = lambda a: pl.BlockSpec(a.shape, lambda i: (0,) * a.ndim)
    kern = functools.partial(_dsa_prompt_kernel, k_sel=k_sel, ch=ch)
    return pl.pallas_call(
        kern,
        grid=(t // rows,),
        in_specs=[
            pl.BlockSpec((IDX_HEADS, rows, IDX_DIM), lambda i: (0, i, 0)),
            pl.BlockSpec((rows, LANES), lambda i: (i, 0)),
            pl.BlockSpec((rows, B_WIDTH), lambda i: (i, 0)),
            full(kit), full(kst), full(vsb), full(bias_d), full(bias_p),
        ],
        out_specs=pl.BlockSpec((rows, B_WIDTH), lambda i: (i, 0)),
        out_shape=jax.ShapeDtypeStruct((t, B_WIDTH), BF16),
        scratch_shapes=[pltpu.VMEM((rows, t), I32), pltpu.VMEM((rows, ch), F32),
                        pltpu.VMEM((DSA_HEADS, rows, 1), F32), pltpu.VMEM((DSA_HEADS, rows, 1), F32),
                        pltpu.VMEM((DSA_HEADS, rows, DSA_HD), F32)],
        compiler_params=_cparams(("parallel",)),
        name="dsa_prompt",
    )(qi4, ikw, qs, kit, kst, vsb, bias_d, bias_p)


def _sample_diff_idx_kernel(pt_ref, *refs, lam_init, g_pages):
    n = g_pages
    k_refs = refs[0:n]
    v_refs = refs[n:2 * n]
    ik_refs = refs[2 * n:3 * n]
    (qd_ref, kd_ref, vd_ref, qib_ref, qif_ref, ki_ref, wi_ref, bl_ref, bn_ref, lamp_ref,
     sub_ref) = refs[3 * n:3 * n + 11]
    o_ref, sc_ref, sn_ref = refs[3 * n + 11:3 * n + 14]
    m_sc, l_sc, acc_sc = refs[3 * n + 14:]
    g = pl.program_id(1)
    ng = pl.num_programs(1)
    nrow = 2 * DIFF_HEADS

    @pl.when(g == 0)
    def _():
        m_sc[...] = jnp.full(m_sc.shape, NEG_INF, F32)
        l_sc[...] = jnp.zeros(l_sc.shape, F32)
        acc_sc[...] = jnp.zeros(acc_sc.shape, F32)

    lane_grp = lax.broadcasted_iota(I32, (nrow, DQ_W), 1) // DIFF_QK
    row_id = lax.broadcasted_iota(I32, (nrow, DQ_W), 0)
    qf = jnp.where(lane_grp == row_id, jnp.broadcast_to(qd_ref[0].astype(F32), (nrow, DQ_W)), 0.0)
    qbd = qf.astype(BF16)

    logits = []
    for j in range(n):
        s = lax.dot_general(qbd, k_refs[j][0].astype(BF16), (((1,), (1,)), ((), ())),
                            preferred_element_type=F32)
        if j == n - 1:
            s = s + jnp.where(g == ng - 1, bl_ref[...], 0.0)
        logits.append(s)
        d = lax.dot_general(qib_ref[0], ik_refs[j][0].astype(BF16), (((1,), (1,)), ((), ())),
                            preferred_element_type=F32)
        srow = jnp.sum(wi_ref[0] * jnp.maximum(d, 0.0), axis=0, keepdims=True)
        off = pl.multiple_of((g * n + j) * LANES, LANES)
        sc_ref[0, :, pl.ds(off, LANES)] = srow
    s = jnp.concatenate(logits, axis=1)
    m_prev = m_sc[...]
    m_new = jnp.maximum(m_prev, jnp.max(s, axis=-1, keepdims=True))
    alpha = jnp.exp(m_prev - m_new)
    p = jnp.exp(s - m_new)
    l_sc[...] = alpha * l_sc[...] + jnp.sum(p, axis=-1, keepdims=True)
    pv = jnp.zeros(acc_sc.shape, F32)
    for j in range(n):
        pv = pv + jnp.dot(p[:, j * LANES:(j + 1) * LANES].astype(BF16), v_refs[j][0].astype(BF16),
                          preferred_element_type=F32)
    acc_sc[...] = alpha * acc_sc[...] + pv
    m_sc[...] = m_new

    @pl.when(g == ng - 1)
    def _():
        s_new = jnp.sum(qf * kd_ref[0], axis=-1, keepdims=True) + bn_ref[:, 0:1]
        m_prev = m_sc[...]
        m_new = jnp.maximum(m_prev, s_new)
        alpha = jnp.exp(m_prev - m_new)
        p_new = jnp.exp(s_new - m_new)
        l_fin = alpha * l_sc[...] + p_new
        acc = alpha * acc_sc[...] + p_new * vd_ref[0]
        lam = _diff_lambda(lamp_ref[...], lam_init)
        for h in range(DIFF_HEADS):
            sl = slice(h * DIFF_V, (h + 1) * DIFF_V)
            o0 = acc[2 * h:2 * h + 1, sl] / l_fin[2 * h:2 * h + 1]
            o1 = acc[2 * h + 1:2 * h + 2, sl] / l_fin[2 * h + 1:2 * h + 2]
            o = o0 - lam * o1
            o_ref[0, :, sl] = (_rms(o, sub_ref[...]) * (1.0 - lam_init)).astype(o_ref.dtype)
        d_new = jnp.sum(qif_ref[0] * ki_ref[0], axis=-1, keepdims=True)
        s_idx = jnp.sum(wi_ref[0][:, 0:1] * jnp.maximum(d_new, 0.0), axis=0, keepdims=True)
        sn_ref[0] = jnp.broadcast_to(s_idx, (1, LANES))


def _sample_diff_idx(page_table, ck, cv, cik, qd, kd, vd, qib, qif, ki, wi_b, bias_last, bias_new,
                     lam_p, subln, lam_init, g_pages):
    db, n_pages = page_table.shape
    page = ck.shape[1]
    past = n_pages * page
    n = g_pages
    ng = n_pages // n

    def page_spec(width, j):
        return pl.BlockSpec((1, page, width), lambda s, g, pt: (pt[s, g * n + j], 0, 0))

    per_s = lambda a: pl.BlockSpec((1,) + a.shape[1:], lambda s, g, pt: (s,) + (0,) * (a.ndim - 1))
    full = lambda a: pl.BlockSpec(a.shape, lambda s, g, pt: (0,) * a.ndim)
    in_specs = ([page_spec(ck.shape[2], j) for j in range(n)]
                + [page_spec(cv.shape[2], j) for j in range(n)]
                + [page_spec(cik.shape[2], j) for j in range(n)]
                + [per_s(qd), per_s(kd), per_s(vd), per_s(qib), per_s(qif), per_s(ki), per_s(wi_b),
                   full(bias_last), full(bias_new), full(lam_p), full(subln)])
    kern = functools.partial(_sample_diff_idx_kernel, lam_init=lam_init, g_pages=n)
    nrow = 2 * DIFF_HEADS
    grid_spec = pltpu.PrefetchScalarGridSpec(
        num_scalar_prefetch=1,
        grid=(db, ng),
        in_specs=in_specs,
        out_specs=[pl.BlockSpec((1, 1, A_WIDTH), lambda s, g, pt: (s, 0, 0)),
                   pl.BlockSpec((1, 1, past), lambda s, g, pt: (s, 0, 0)),
                   pl.BlockSpec((1, 1, LANES), lambda s, g, pt: (s, 0, 0))],
        scratch_shapes=[pltpu.VMEM((nrow, 1), F32), pltpu.VMEM((nrow, 1), F32),
                        pltpu.VMEM((nrow, A_WIDTH), F32)],
    )
    return pl.pallas_call(
        kern,
        grid_spec=grid_spec,
        out_shape=[jax.ShapeDtypeStruct((db, 1, A_WIDTH), BF16),
                   jax.ShapeDtypeStruct((db, 1, past), F32),
                   jax.ShapeDtypeStruct((db, 1, LANES), F32)],
        compiler_params=_cparams(("parallel", "arbitrary")),
        name="sample_diff_idx",
    )(page_table, *([ck] * n), *([cv] * n), *([cik] * n), qd, kd, vd, qib, qif, ki, wi_b,
      bias_last, bias_new, lam_p, subln)


def _sample_select_kernel(s_ref, o_ref, keys_sc, *, k_sel):
    rows, nk = s_ref.shape
    n_tiles = nk // LANES

    def to_keys(j, _):
        off = pl.multiple_of(j * LANES, LANES)
        keys_sc[:, pl.ds(off, LANES)] = _sort_key(s_ref[:, pl.ds(off, LANES)])
        return 0

    lax.fori_loop(0, n_tiles, to_keys, 0)
    t = _kth_largest_key(keys_sc, n_tiles, k_sel, rows)
    t_b = jnp.broadcast_to(t, (rows, LANES))
    c_gt = _count_tiles(keys_sc, n_tiles, lambda kt: kt > t_b, rows)
    m_b = jnp.broadcast_to(float(k_sel) - c_gt, (rows, LANES))
    tri, ones = _tie_constants()

    def emit(j, carry):
        off = pl.multiple_of(j * LANES, LANES)
        sel, carry = _select_tile(keys_sc[:, pl.ds(off, LANES)], t_b, m_b, carry, tri, ones)
        o_ref[:, pl.ds(off, LANES)] = jnp.where(sel > 0.5, 0.0, NEG_INF)
        return carry

    lax.fori_loop(0, n_tiles, emit, jnp.zeros((rows, LANES), F32))


def _sample_select(scores, k_sel):
    rows, nk = scores.shape
    return pl.pallas_call(
        functools.partial(_sample_select_kernel, k_sel=k_sel),
        out_shape=jax.ShapeDtypeStruct((rows, nk), F32),
        scratch_shapes=[pltpu.VMEM((rows, nk), I32)],
        compiler_params=pltpu.CompilerParams(vmem_limit_bytes=VMEM_LIMIT),
        name="sample_select",
    )(scores)


def _sample_dsa_kernel(pt_ref, *refs, g_pages):
    n = g_pages
    k_refs = refs[0:n]
    v_refs = refs[n:2 * n]
    qs_ref, ks_ref, vs_ref, msk_ref, bl_ref, bn_ref = refs[2 * n:2 * n + 6]
    o_ref = refs[2 * n + 6]
    m_sc, l_sc, acc_sc = refs[2 * n + 7:]
    g = pl.program_id(1)
    ng = pl.num_programs(1)

    @pl.when(g == 0)
    def _():
        m_sc[...] = jnp.full(m_sc.shape, NEG_INF, F32)
        l_sc[...] = jnp.zeros(l_sc.shape, F32)
        acc_sc[...] = jnp.zeros(acc_sc.shape, F32)

    q = qs_ref[0]
    logits = []
    for j in range(n):
        s = lax.dot_general(q, k_refs[j][0].astype(BF16), (((1,), (1,)), ((), ())),
                            preferred_element_type=F32)
        if j == n - 1:
            s = s + jnp.where(g == ng - 1, bl_ref[...], 0.0)
        off = pl.multiple_of((g * n + j) * LANES, LANES)
        logits.append(s + msk_ref[0, :, pl.ds(off, LANES)])
    s = jnp.concatenate(logits, axis=1)
    m_prev = m_sc[...]
    m_new = jnp.maximum(m_prev, jnp.max(s, axis=-1, keepdims=True))
    alpha = jnp.exp(m_prev - m_new)
    p = jnp.exp(s - m_new)
    l_sc[...] = alpha * l_sc[...] + jnp.sum(p, axis=-1, keepdims=True)
    pv = jnp.zeros(acc_sc.shape, F32)
    for j in range(n):
        pv = pv + jnp.dot(p[:, j * LANES:(j + 1) * LANES].astype(BF16), v_refs[j][0].astype(BF16),
                          preferred_element_type=F32)
    acc_sc[...] = alpha * acc_sc[...] + pv
    m_sc[...] = m_new

    @pl.when(g == ng - 1)
    def _():
        past = ng * n * LANES
        s_new = (jnp.sum(q.astype(F32) * ks_ref[0], axis=-1, keepdims=True) + bn_ref[:, 0:1]
                 + msk_ref[0, :, past:past + 1])
        m_prev = m_sc[...]
        m_new = jnp.maximum(m_prev, s_new)
        alpha = jnp.exp(m_prev - m_new)
        p_new = jnp.exp(s_new - m_new)
        l_fin = alpha * l_sc[...] + p_new
        acc = alpha * acc_sc[...] + p_new * vs_ref[0]
        o_ref[0] = (acc / l_fin).astype(o_ref.dtype)


def _sample_dsa(page_table, ck, cv, qs, ks, vs, mask, bias_last, bias_new, g_pages):
    db, n_pages = page_table.shape
    page = ck.shape[1]
    n = g_pages
    ng = n_pages // n

    def page_spec(width, j):
        return pl.BlockSpec((1, page, width), lambda s, g, pt: (pt[s, g * n + j], 0, 0))

    per_s = lambda a: pl.BlockSpec((1,) + a.shape[1:], lambda s, g, pt: (s,) + (0,) * (a.ndim - 1))
    full = lambda a: pl.BlockSpec(a.shape, lambda s, g, pt: (0,) * a.ndim)
    in_specs = ([page_spec(ck.shape[2], j) for j in range(n)]
                + [page_spec(cv.shape[2], j) for j in range(n)]
                + [per_s(qs), per_s(ks), per_s(vs), per_s(mask), full(bias_last), full(bias_new)])
    grid_spec = pltpu.PrefetchScalarGridSpec(
        num_scalar_prefetch=1,
        grid=(db, ng),
        in_specs=in_specs,
        out_specs=pl.BlockSpec((1, DSA_HEADS, DSA_HD), lambda s, g, pt: (s, 0, 0)),
        scratch_shapes=[pltpu.VMEM((DSA_HEADS, 1), F32), pltpu.VMEM((DSA_HEADS, 1), F32),
                        pltpu.VMEM((DSA_HEADS, DSA_HD), F32)],
    )
    return pl.pallas_call(
        functools.partial(_sample_dsa_kernel, g_pages=n),
        grid_spec=grid_spec,
        out_shape=jax.ShapeDtypeStruct((db, DSA_HEADS, DSA_HD), BF16),
        compiler_params=_cparams(("parallel", "arbitrary")),
        name="sample_dsa",
    )(page_table, *([ck] * n), *([cv] * n), qs, ks, vs, mask, bias_last, bias_new)


def _tail_a_kernel(x_ref, da_ref, sa_ref, gate_ref, wa_ref, wb_ref, wo_ref, n2_ref, x1_ref, h2_ref):
    d = x_ref.shape[1]
    ya = jnp.dot(da_ref[...], wa_ref[...], preferred_element_type=F32)
    yb = jnp.dot(sa_ref[...], wb_ref[...], preferred_element_type=F32)
    merged = gate_ref[:, :d].astype(F32) * ya + gate_ref[:, d:].astype(F32) * yb
    x1 = x_ref[...] + jnp.dot(merged.astype(BF16), wo_ref[...], preferred_element_type=F32)
    x1_ref[...] = x1
    h2_ref[...] = _rms(x1, n2_ref[...]).astype(BF16)


def _tail_a(x, da, sa, gate, wa, wb, wo, n2):
    t, d = x.shape
    tb = min(256, t)
    row = lambda n: pl.BlockSpec((tb, n), lambda i: (i, 0))
    full = lambda a: pl.BlockSpec(a.shape, lambda i: (0,) * a.ndim)
    return pl.pallas_call(
        _tail_a_kernel,
        grid=(t // tb,),
        in_specs=[row(d), row(da.shape[1]), row(sa.shape[1]), row(2 * d), full(wa), full(wb),
                  full(wo), full(n2)],
        out_specs=[row(d), row(d)],
        out_shape=[jax.ShapeDtypeStruct((t, d), F32), jax.ShapeDtypeStruct((t, d), BF16)],
        compiler_params=_cparams(("parallel",)),
        name="tail_a",
    )(x, da, sa, gate, wa, wb, wo, n2)


def _top_rows(work, n_top, store):
    rows = work.shape[0]
    iota = lax.broadcasted_iota(I32, work.shape, 0).astype(F32)
    mx = None
    for r in range(n_top):
        mx = jnp.max(work, axis=0, keepdims=True)
        store(r, mx)
        first = jnp.min(jnp.where(work == mx, iota, float(rows)), axis=0, keepdims=True)
        work = jnp.where(iota == first, -jnp.inf, work)
    return mx


def _staircase_pairs():
    return [(i, j) for i in range(PEER_TOPK) for j in range(PEER_TOPK // (i + 1))]


def _peer_route_kernel(h_ref, wq_ref, keys_ref, s1_ref, s2_ref, a_ref, b_ref, th_ref,
                       q_sc, top_sc, cand_sc):
    tb = h_ref.shape[0]
    q_sc[...] = jnp.dot(h_ref[...], wq_ref[...], preferred_element_type=F32).astype(BF16)
    pairs = _staircase_pairs()
    n_cand = cand_sc.shape[0]
    cand_sc[len(pairs):, :] = jnp.full((n_cand - len(pairs), tb), -jnp.inf, F32)
    s_refs = (s1_ref, s2_ref)

    def head(h, _):
        expo = []
        for c in range(2):
            off = pl.multiple_of((h * 2 + c) * PEER_KH, PEER_KH)
            s = lax.dot_general(keys_ref[c, h], q_sc[:, pl.ds(off, PEER_KH)],
                                (((1,), (1,)), ((), ())), preferred_element_type=F32)
            s_refs[c][h] = s

            def store(r, row, c=c):
                top_sc[c, r:r + 1, :] = row

            _top_rows(s, PEER_TOPK, store)
            expo.append(jnp.exp(s - top_sc[c, 0:1, :]))
        for r, (i, j) in enumerate(pairs):
            cand_sc[r:r + 1, :] = top_sc[0, i:i + 1, :] + top_sc[1, j:j + 1, :]
        cs = cand_sc[...]
        theta = _top_rows(cs, PEER_TOPK, lambda r, row: None)
        top_sum = top_sc[0, 0:1, :] + top_sc[1, 0:1, :]
        z = jnp.sum(jnp.where(cs >= theta, jnp.exp(cs - top_sum), 0.0), axis=0, keepdims=True)
        a_ref[h] = expo[0]
        b_ref[h] = expo[1] / z
        th_ref[pl.ds(h, 1), :] = theta
        return 0

    lax.fori_loop(0, PEER_HEADS, head, 0)


def _peer_route(h2, wq, keys):
    t, d = h2.shape
    tb = min(256, t)
    full = lambda a: pl.BlockSpec(a.shape, lambda i: (0,) * a.ndim)
    big = pl.BlockSpec((PEER_HEADS, PEER_NKEYS, tb), lambda i: (0, 0, i))
    big_shape = jax.ShapeDtypeStruct((PEER_HEADS, PEER_NKEYS, t), F32)
    n_cand = -(-len(_staircase_pairs()) // 8) * 8
    return pl.pallas_call(
        _peer_route_kernel,
        grid=(t // tb,),
        in_specs=[pl.BlockSpec((tb, d), lambda i: (i, 0)), full(wq), full(keys)],
        out_specs=[big, big, big, big, pl.BlockSpec((PEER_HEADS, tb), lambda i: (0, i))],
        out_shape=[big_shape, big_shape, big_shape, big_shape,
                   jax.ShapeDtypeStruct((PEER_HEADS, t), F32)],
        scratch_shapes=[pltpu.VMEM((tb, wq.shape[1]), BF16), pltpu.VMEM((2, PEER_TOPK, tb), F32),
                        pltpu.VMEM((n_cand, tb), F32)],
        compiler_params=_cparams(("parallel",)),
        name="peer_route",
    )(h2, wq, keys)


def _gelu(x):
    return 0.5 * x * (1.0 + lax.erf(x * math.sqrt(0.5)))


def _peer_dense_kernel(h_ref, eu_ref, evt_ref, s1_ref, s2_ref, a_ref, b_ref, th_ref, o_ref, p_sc):
    e = pl.program_id(1)
    ec = eu_ref.shape[0]
    n_i1 = ec // PEER_NKEYS

    @pl.when(e == 0)
    def _():
        o_ref[...] = jnp.zeros(o_ref.shape, F32)

    act = _gelu(lax.dot_general(eu_ref[...], h_ref[...], (((1,), (1,)), ((), ())),
                                preferred_element_type=F32))
    for jj in range(n_i1):
        i1 = e * n_i1 + jj
        w = jnp.zeros((PEER_NKEYS, h_ref.shape[0]), F32)
        for h in range(PEER_HEADS):
            ssum = s1_ref[h, pl.ds(i1, 1), :] + s2_ref[h]
            w = w + jnp.where(ssum >= th_ref[h:h + 1, :], a_ref[h, pl.ds(i1, 1), :] * b_ref[h], 0.0)
        sl = slice(jj * PEER_NKEYS, (jj + 1) * PEER_NKEYS)
        p_sc[sl, :] = (w * act[sl, :]).astype(BF16)
    o_ref[...] += jnp.dot(evt_ref[...], p_sc[...], preferred_element_type=F32)


def _peer_dense(h2, eu, evt, s1, s2, a, b, theta, tb, ec):
    t, d = h2.shape
    n_exp = eu.shape[0]
    big = pl.BlockSpec((PEER_HEADS, PEER_NKEYS, tb), lambda i, e: (0, 0, i))
    return pl.pallas_call(
        _peer_dense_kernel,
        grid=(t // tb, n_exp // ec),
        in_specs=[pl.BlockSpec((tb, d), lambda i, e: (i, 0)),
                  pl.BlockSpec((ec, d), lambda i, e: (e, 0)),
                  pl.BlockSpec((d, ec), lambda i, e: (0, e)),
                  big, big, big, big,
                  pl.BlockSpec((PEER_HEADS, tb), lambda i, e: (0, i))],
        out_specs=pl.BlockSpec((d, tb), lambda i, e: (0, i)),
        out_shape=jax.ShapeDtypeStruct((d, t), F32),
        scratch_shapes=[pltpu.VMEM((ec, tb), BF16)],
        compiler_params=_cparams(("parallel", "arbitrary")),
        name="peer_dense",
    )(h2, eu, evt, s1, s2, a, b, theta)


def _tail_b_kernel(x1_ref, ff_ref, p_ref, pg_ref, pp_ref, y_ref):
    x2 = x1_ref[...] + ff_ref[...]
    gate = jax.nn.sigmoid(jnp.dot(x2.astype(BF16), pg_ref[...], preferred_element_type=F32))
    y_ref[...] = x2 + gate * jnp.dot(p_ref[...].astype(BF16), pp_ref[...], preferred_element_type=F32)


def _tail_b(x1, ff, p, pg, pp):
    t, d = x1.shape
    tb = min(256, t)
    row = lambda n: pl.BlockSpec((tb, n), lambda i: (i, 0))
    full = lambda a: pl.BlockSpec(a.shape, lambda i: (0,) * a.ndim)
    return pl.pallas_call(
        _tail_b_kernel,
        grid=(t // tb,),
        in_specs=[row(d), row(d), row(p.shape[1]), full(pg), full(pp)],
        out_specs=row(d),
        out_shape=jax.ShapeDtypeStruct((t, d), F32),
        compiler_params=_cparams(("parallel",)),
        name="tail_b",
    )(x1, ff, p, pg, pp)


def _layer_tail(x, da, sa, gate, p, w):
    x1, h2 = _tail_a(x, da, sa, gate, w["wa"], w["wb"], w["wo"], w["n2"])
    s1, s2, a, b, theta = _peer_route(h2, w["wq"], w["keys"])
    t = x.shape[0]
    tb = 512 if t % 512 == 0 else min(LANES, t)
    fft = _peer_dense(h2, w["eu"], w["evt"], s1, s2, a, b, theta, tb, 512)
    return _tail_b(x1, fft.T, p, w["pg"], w["pp"])


def _bias_tiles(tab, blk):
    d = np.arange(blk)[:, None] - np.arange(blk)[None, :]
    tab = tab - tab[N_BUCKETS - 1:N_BUCKETS]
    diag = jnp.moveaxis(tab[_bucket_of(np.maximum(d, 0))], -1, 0)
    prev = jnp.moveaxis(tab[_bucket_of(d + blk)], -1, 0)
    return diag, prev


def _decode_bias(tab, page, rep):
    tab = tab - tab[N_BUCKETS - 1:N_BUCKETS]
    last = jnp.repeat(tab[_bucket_of(page - np.arange(page))].T, rep, axis=0)
    new = jnp.repeat(jnp.broadcast_to(tab[0][:, None], (tab.shape[1], LANES)), rep, axis=0)
    return last, new


def _tile_lanes(g, reps):
    return jnp.tile(g.reshape(1, -1), (1, reps))


def _group_mean_matrix(width, group):
    idx = np.arange(width) // group
    return jnp.asarray((idx[:, None] == idx[None, :]).astype(np.float32) / group, BF16)


def kernel(x_prompt, x_sample, p_prompt, p_sample, cache_diff_k, cache_diff_v, cache_dsa_k, cache_dsa_v, cache_idx_k, page_table, rel_bias, norm1_g, w_in, diff_q_norm, diff_k_norm, diff_lambda_p, diff_subln, dsa_q_norm, dsa_k_norm, w_branch_a, w_branch_b, w_out, norm2_g, peer_wq, peer_keys, peer_u, peer_v, ple_gate, ple_proj):
    depth = w_in.shape[0]
    batch, seq, d_model = x_prompt.shape
    db, dec_seq, _ = x_sample.shape
    assert batch == 1 and dec_seq == 1
    n_pages = page_table.shape[1]
    page = cache_diff_k.shape[2]
    past = n_pages * page
    assert page == LANES and seq % 512 == 0
    k_sel_p = min(TOPK_MAX, seq // 4)
    k_sel_s = min(TOPK_MAX, (past + dec_seq) // 4)
    blk = 256
    g_pages = 8 if n_pages % 8 == 0 else 1

    tab_d = rel_bias[:, :DIFF_HEADS]
    tab_s = rel_bias[:, DIFF_HEADS:]
    bias_dd, bias_dp = _bias_tiles(tab_d, blk)
    bias_sd, bias_sp = _bias_tiles(tab_s, LANES)
    dec_d_last, dec_d_new = _decode_bias(tab_d, page, 2)
    dec_s_last, dec_s_new = _decode_bias(tab_s, page, 1)
    g64 = _group_mean_matrix(DQ_W, DIFF_QK)
    g128 = _group_mean_matrix(B_WIDTH, DSA_HD)

    xp = x_prompt.reshape(seq, d_model)
    xs = x_sample.reshape(db, d_model)
    outs_p = {k: [] for k in ("dk", "dv", "sk", "sv", "ik")}
    outs_s = {k: [] for k in ("dk", "dv", "sk", "sv", "ik")}
    for l in range(depth):
        lam_init = 0.8 - 0.6 * math.exp(-0.3 * l)
        n_split = OFF_IKW + IKW_USED
        w_pad = jnp.concatenate(
            [w_in[l][:, :n_split], jnp.zeros((d_model, LANES - IKW_USED), F32), w_in[l][:, n_split:]],
            axis=1).astype(BF16)
        g1 = norm1_g[l].reshape(1, d_model)
        dqn = _tile_lanes(diff_q_norm[l], DQ_W // DIFF_QK)
        dkn = _tile_lanes(diff_k_norm[l], DQ_W // DIFF_QK)
        sqn = _tile_lanes(dsa_q_norm[l], DSA_HEADS)
        skn = dsa_k_norm[l].reshape(1, DSA_HD)
        lam_p = diff_lambda_p[l]
        subln = diff_subln[l].reshape(1, DIFF_V)
        tail_w = dict(
            wa=w_branch_a[l].astype(BF16), wb=w_branch_b[l].astype(BF16), wo=w_out[l].astype(BF16),
            n2=norm2_g[l].reshape(1, d_model), wq=peer_wq[l].astype(BF16),
            keys=peer_keys[l].astype(BF16), eu=peer_u[l].astype(BF16),
            evt=peer_v[l].T.astype(BF16), pg=ple_gate[l].astype(BF16), pp=ple_proj[l].astype(BF16))

        (qd, kd, kdb, vd, vdb, qs, ks, ksb, vs, vsb, qi, ikw, kib, gate) = _proj(
            xp, g1, w_pad, dqn, dkn, sqn, skn, g64, g128)
        q4 = qd.reshape(seq, DIFF_HEADS, 2, DIFF_QK).transpose(1, 2, 0, 3)
        k4 = kdb.reshape(seq, DIFF_HEADS, 2, DIFF_QK).transpose(1, 2, 0, 3)
        v3 = vdb.reshape(seq, DIFF_HEADS, DIFF_V).transpose(1, 0, 2)
        da = _diff_attn(q4, k4, v3, bias_dd, bias_dp, lam_p, subln, lam_init, min(blk, seq))
        qi4 = qi.reshape(seq, IDX_HEADS, IDX_DIM).transpose(1, 0, 2)
        sa = _dsa_prompt(qi4, ikw, qs, kib[:, :IDX_DIM].T, ksb.T, vsb, bias_sd, bias_sp, k_sel_p, 512)
        xp = _layer_tail(xp, da, sa, gate, p_prompt[l, 0], tail_w)
        outs_p["dk"].append(kd.reshape(batch, seq, DIFF_HEADS, 2, DIFF_QK))
        outs_p["dv"].append(vd.reshape(batch, seq, DIFF_HEADS, DIFF_V))
        outs_p["sk"].append(ks.reshape(batch, seq, DSA_HD))
        outs_p["sv"].append(vs.reshape(batch, seq, DSA_HD))
        outs_p["ik"].append(ikw[:, :IDX_DIM].reshape(batch, seq, IDX_DIM))

        (qd2, kd2, _, vd2, _, qs2, ks2, _, vs2, _, qi2, ikw2, _, gate2) = _proj(
            xs, g1, w_pad, dqn, dkn, sqn, skn, g64, g128)
        ki2 = ikw2[:, :IDX_DIM]
        wi2 = ikw2[:, IDX_DIM:IDX_DIM + IDX_HEADS]
        n_phys = cache_diff_k.shape[1]
        da_s, sc_past, sc_new = _sample_diff_idx(
            page_table,
            cache_diff_k[l].reshape(n_phys, page, DQ_W),
            cache_diff_v[l].reshape(n_phys, page, A_WIDTH),
            cache_idx_k[l],
            qd2.reshape(db, 1, DQ_W), kd2.reshape(db, 1, DQ_W), vd2.reshape(db, 1, A_WIDTH),
            qi2.reshape(db, IDX_HEADS, IDX_DIM), qi2.astype(F32).reshape(db, IDX_HEADS, IDX_DIM),
            ki2.reshape(db, 1, IDX_DIM),
            jnp.broadcast_to(wi2[:, :, None], (db, IDX_HEADS, LANES)),
            dec_d_last, dec_d_new, lam_p, subln, lam_init, g_pages)
        scores = jnp.concatenate(
            [sc_past.reshape(db, past), sc_new.reshape(db, LANES)[:, :1],
             jnp.full((db, LANES - 1), -jnp.inf, F32)], axis=1)
        mask = _sample_select(scores, k_sel_s)
        sa_s = _sample_dsa(
            page_table, cache_dsa_k[l], cache_dsa_v[l],
            qs2.reshape(db, DSA_HEADS, DSA_HD), ks2.reshape(db, 1, DSA_HD), vs2.reshape(db, 1, DSA_HD),
            mask.reshape(db, 1, past + LANES), dec_s_last, dec_s_new, g_pages)
        xs = _layer_tail(xs, da_s.reshape(db, A_WIDTH), sa_s.reshape(db, B_WIDTH), gate2,
                         p_sample[l, :, 0], tail_w)
        outs_s["dk"].append(kd2.reshape(db, dec_seq, DIFF_HEADS, 2, DIFF_QK))
        outs_s["dv"].append(vd2.reshape(db, dec_seq, DIFF_HEADS, DIFF_V))
        outs_s["sk"].append(ks2.reshape(db, dec_seq, DSA_HD))
        outs_s["sv"].append(vs2.reshape(db, dec_seq, DSA_HD))
        outs_s["ik"].append(ki2.reshape(db, dec_seq, IDX_DIM))

    st = jnp.stack
    return (xp.reshape(batch, seq, d_model), xs.reshape(db, dec_seq, d_model),
            st(outs_p["dk"]), st(outs_p["dv"]), st(outs_p["sk"]), st(outs_p["sv"]), st(outs_p["ik"]),
            st(outs_s["dk"]), st(outs_s["dv"]), st(outs_s["sk"]), st(outs_s["sv"]), st(outs_s["ik"]))
```

```python
import functools
import math

import numpy as np
import jax
import jax.numpy as jnp
from jax import lax
from jax.experimental import pallas as pl
from jax.experimental.pallas import tpu as pltpu

F32 = jnp.float32
BF16 = jnp.bfloat16
I32 = jnp.int32

DIFF_HEADS = 4
DIFF_QK = 64
DIFF_V = 128
DSA_HEADS = 4
DSA_HD = 128
IDX_HEADS = 4
IDX_DIM = 64
TOPK_MAX = 256
N_BUCKETS = 32
MAX_EXACT = 16
MAX_DIST = 128
PEER_HEADS = 8
PEER_NKEYS = 128
PEER_KH = 128
PEER_TOPK = 16
RMS_EPS = 1e-6
NEG_INF = -1e30

LANES = 128
VMEM_LIMIT = 56 * 1024 * 1024

DIFF_BLK = 512
DSA_ROWS = 256
DSA_CHUNK = 1024
PEER_TB = 512
PEER_EC = 512

A_WIDTH = DIFF_HEADS * DIFF_V
B_WIDTH = DSA_HEADS * DSA_HD
DQ_W = DIFF_HEADS * 2 * DIFF_QK
IQ_W = IDX_HEADS * IDX_DIM

OFF_DQ = 0
OFF_DK = OFF_DQ + DQ_W
OFF_DV = OFF_DK + DQ_W
OFF_SQ = OFF_DV + A_WIDTH
OFF_SK = OFF_SQ + B_WIDTH
OFF_SV = OFF_SK + DSA_HD
OFF_IQ = OFF_SV + DSA_HD
OFF_IKW = OFF_IQ + IQ_W
OFF_G = OFF_IKW + LANES
IKW_USED = IDX_DIM + IDX_HEADS


def _bucket_table():
    d = np.arange(MAX_DIST)
    nf = np.maximum(d, 1).astype(np.float64)
    val = np.log(nf / MAX_EXACT) / math.log(MAX_DIST / MAX_EXACT) * (N_BUCKETS - MAX_EXACT)
    frac = np.abs(val - np.round(val))
    assert np.all(frac[MAX_EXACT + 1:] > 1e-4)
    large = MAX_EXACT + np.floor(val + 1e-9).astype(np.int64)
    return np.where(d < MAX_EXACT, d, np.minimum(large, N_BUCKETS - 1)).astype(np.int32)


_BUCKET = _bucket_table()


def _bucket_of(dist):
    dist = np.asarray(dist)
    return np.where(dist < MAX_DIST, _BUCKET[np.clip(dist, 0, MAX_DIST - 1)], N_BUCKETS - 1)


def _cparams(sem):
    return pltpu.CompilerParams(dimension_semantics=sem, vmem_limit_bytes=VMEM_LIMIT)


def _rms(x, gain):
    return x * lax.rsqrt(jnp.mean(x * x, axis=-1, keepdims=True) + RMS_EPS) * gain


def _proj_kernel(x_ref, g1_ref, w_ref, dqn_ref, dkn_ref, sqn_ref, skn_ref, g64_ref, g128_ref,
                 qd_ref, kd_ref, kdb_ref, vd_ref, vdb_ref, qs_ref, ks_ref, ksb_ref,
                 vs_ref, vsb_ref, qi_ref, ikw_ref, kib_ref, gate_ref):
    hb = _rms(x_ref[...], g1_ref[...]).astype(BF16)

    def seg(off, n):
        return jnp.dot(hb, w_ref[:, off:off + n], preferred_element_type=F32)

    def group_norm(z, gmat, gain):
        sq = z * z
        hi = sq.astype(BF16)
        lo = (sq - hi.astype(F32)).astype(BF16)
        ms = (jnp.dot(hi, gmat, preferred_element_type=F32)
              + jnp.dot(lo, gmat, preferred_element_type=F32))
        return z * lax.rsqrt(ms + RMS_EPS) * gain

    g64 = g64_ref[...]
    g128 = g128_ref[...]
    qd = group_norm(seg(OFF_DQ, DQ_W), g64, dqn_ref[...])
    qd_ref[...] = (qd * (DIFF_QK ** -0.5)).astype(BF16)
    kd = group_norm(seg(OFF_DK, DQ_W), g64, dkn_ref[...])
    kd_ref[...] = kd
    kdb_ref[...] = kd.astype(BF16)
    vd = seg(OFF_DV, A_WIDTH)
    vd_ref[...] = vd
    vdb_ref[...] = vd.astype(BF16)
    qs = group_norm(seg(OFF_SQ, B_WIDTH), g128, sqn_ref[...])
    qs_ref[...] = (qs * (DSA_HD ** -0.5)).astype(BF16)
    ks = _rms(seg(OFF_SK, DSA_HD), skn_ref[...])
    ks_ref[...] = ks
    ksb_ref[...] = ks.astype(BF16)
    vs = seg(OFF_SV, DSA_HD)
    vs_ref[...] = vs
    vsb_ref[...] = vs.astype(BF16)
    qi_ref[...] = seg(OFF_IQ, IQ_W).astype(BF16)
    ikw = seg(OFF_IKW, LANES)
    ikw_ref[...] = ikw
    kib_ref[...] = ikw.astype(BF16)
    d_model = x_ref.shape[1]
    for half in range(2):
        zg = seg(OFF_G + half * d_model, d_model)
        gate_ref[:, half * d_model:(half + 1) * d_model] = jax.nn.sigmoid(zg).astype(BF16)


def _proj(x, g1, w_pad, dqn, dkn, sqn, skn, g64, g128):
    t, d = x.shape
    tb = min(256, t)
    n_in = w_pad.shape[1]
    row = lambda n: pl.BlockSpec((tb, n), lambda i: (i, 0))
    full = lambda a: pl.BlockSpec(a.shape, lambda i: (0,) * a.ndim)
    outs = [
        (DQ_W, BF16),
        (DQ_W, F32), (DQ_W, BF16),
        (A_WIDTH, F32), (A_WIDTH, BF16),
        (B_WIDTH, BF16),
        (DSA_HD, F32), (DSA_HD, BF16),
        (DSA_HD, F32), (DSA_HD, BF16),
        (IQ_W, BF16),
        (LANES, F32), (LANES, BF16),
        (2 * d, BF16),
    ]
    return pl.pallas_call(
        _proj_kernel,
        grid=(t // tb,),
        in_specs=[row(d), full(g1), full(w_pad), full(dqn), full(dkn), full(sqn), full(skn),
                  full(g64), full(g128)],
        out_specs=[row(n) for n, _ in outs],
        out_shape=[jax.ShapeDtypeStruct((t, n), dt) for n, dt in outs],
        compiler_params=_cparams(("parallel",)),
        name="proj",
    )(x, g1, w_pad, dqn, dkn, sqn, skn, g64, g128)


def _diff_lambda(lp, lam_init):
    e1 = jnp.exp(jnp.sum(lp[0:1] * lp[1:2], axis=-1, keepdims=True))
    e2 = jnp.exp(jnp.sum(lp[2:3] * lp[3:4], axis=-1, keepdims=True))
    return e1 - e2 + lam_init


def _diff_attn_kernel(q_ref, kt_ref, v_ref, bd_ref, bp_ref, lamp_ref, sub_ref, o_ref,
                      m_sc, l_sc, acc_sc, *, lam_init):
    qi = pl.program_id(1)
    blk = q_ref.shape[2]
    n_rep = blk // LANES
    m_sc[...] = jnp.full(m_sc.shape, NEG_INF, F32)
    l_sc[...] = jnp.zeros(l_sc.shape, F32)
    acc_sc[...] = jnp.zeros(acc_sc.shape, F32)

    def chunk(j, kind):
        off = pl.multiple_of(j * blk, blk)
        v = v_ref[0, pl.ds(off, blk), :]
        for c in range(2):
            s = jnp.dot(q_ref[0, c], kt_ref[0, c, :, pl.ds(off, blk)], preferred_element_type=F32)
            if kind == 1:
                s = s + bp_ref[0]
            if kind == 2:
                r = lax.broadcasted_iota(I32, (blk, blk), 0)
                cc = lax.broadcasted_iota(I32, (blk, blk), 1)
                s = jnp.where(r >= cc, s + bd_ref[0], NEG_INF)
            m_prev = m_sc[c]
            m_new = jnp.maximum(m_prev, jnp.max(s, axis=-1, keepdims=True))
            alpha = jnp.exp(m_prev - m_new)
            p = jnp.exp(s - jnp.tile(m_new, (1, n_rep)))
            l_sc[c] = alpha * l_sc[c] + jnp.sum(p, axis=-1, keepdims=True)
            acc_sc[c] = alpha * acc_sc[c] + jnp.dot(p.astype(BF16), v, preferred_element_type=F32)
            m_sc[c] = m_new

    def far(j, carry):
        chunk(j, 0)
        return carry

    lax.fori_loop(0, jnp.maximum(qi - 1, 0), far, 0)
    pl.when(qi >= 1)(lambda: chunk(qi - 1, 1))
    chunk(qi, 2)
    lam = _diff_lambda(lamp_ref[...], lam_init)
    o = acc_sc[0] / l_sc[0] - lam * (acc_sc[1] / l_sc[1])
    o_ref[...] = (_rms(o, sub_ref[...]) * (1.0 - lam_init)).astype(o_ref.dtype)


def _diff_attn(q4, kt4, v3, bias_d, bias_p, lam_p, subln, lam_init, blk):
    h, _, t, dqk = q4.shape
    dv = v3.shape[2]
    assert dv == LANES
    kern = functools.partial(_diff_attn_kernel, lam_init=lam_init)
    return pl.pallas_call(
        kern,
        grid=(h, t // blk),
        in_specs=[
            pl.BlockSpec((1, 2, blk, dqk), lambda hh, qi: (hh, 0, qi, 0)),
            pl.BlockSpec((1, 2, dqk, t), lambda hh, qi: (hh, 0, 0, 0)),
            pl.BlockSpec((1, t, dv), lambda hh, qi: (hh, 0, 0)),
            pl.BlockSpec((1, blk, blk), lambda hh, qi: (hh, 0, 0)),
            pl.BlockSpec((1, blk, blk), lambda hh, qi: (hh, 0, 0)),
            pl.BlockSpec(lam_p.shape, lambda hh, qi: (0, 0)),
            pl.BlockSpec(subln.shape, lambda hh, qi: (0, 0)),
        ],
        out_specs=pl.BlockSpec((blk, dv), lambda hh, qi: (qi, hh)),
        out_shape=jax.ShapeDtypeStruct((t, h * dv), BF16),
        scratch_shapes=[pltpu.VMEM((2, blk, LANES), F32), pltpu.VMEM((2, blk, LANES), F32),
                        pltpu.VMEM((2, blk, dv), F32)],
        compiler_params=_cparams(("parallel", "arbitrary")),
        name="diff_attn",
    )(q4, kt4, v3, bias_d, bias_p, lam_p, subln)


def _sort_key(s):
    s = jnp.where(s == 0.0, 0.0, s)
    bits = pltpu.bitcast(s, I32)
    return jnp.where(bits < 0, bits ^ jnp.int32(0x7FFFFFFF), bits)


ROW_GROUP = 64


def _count_keys(keys_sc, r0, rg, n_ch, ch, pred):
    def body(c, acc):
        off = pl.multiple_of(c * ch, ch)
        for j in range(ch // LANES):
            kt = keys_sc[r0:r0 + rg, pl.ds(off + j * LANES, LANES)]
            acc = acc + jnp.where(pred(kt), 1.0, 0.0)
        return acc

    acc = lax.fori_loop(0, n_ch, body, jnp.zeros((rg, LANES), F32))
    return jnp.sum(acc, axis=-1, keepdims=True)


def _topk_threshold(keys_sc, rows, n_ch, ch, k_sel):
    ts, ms = [], []
    for r0 in range(0, rows, ROW_GROUP):
        rg = min(ROW_GROUP, rows - r0)

        def bis(state, r0=r0, rg=rg):
            i, t, done = state
            cand = t ^ jnp.left_shift(jnp.int32(1), 31 - i)
            cand_b = jnp.broadcast_to(cand, (rg, LANES))
            cnt = _count_keys(keys_sc, r0, rg, n_ch, ch, lambda kt: kt >= cand_b)
            t = jnp.where(done > 0.5, t, jnp.where(cnt >= float(k_sel), cand, t))
            return i + 1, t, jnp.where(cnt == float(k_sel), 1.0, done)

        def unfinished(state):
            i, _, done = state
            return jnp.logical_and(i < 32, jnp.min(done) < 0.5)

        _, t, _ = lax.while_loop(unfinished, bis, (jnp.int32(0), jnp.full((rg, 1), -2 ** 31, I32),
                                                   jnp.zeros((rg, 1), F32)))
        t_b = jnp.broadcast_to(t, (rg, LANES))
        c_gt = _count_keys(keys_sc, r0, rg, n_ch, ch, lambda kt: kt > t_b)
        ts.append(t_b)
        ms.append(jnp.broadcast_to(float(k_sel) - c_gt, (rg, LANES)))
    return jnp.concatenate(ts, axis=0), jnp.concatenate(ms, axis=0)


def _tie_constants():
    r = lax.broadcasted_iota(I32, (LANES, LANES), 0)
    c = lax.broadcasted_iota(I32, (LANES, LANES), 1)
    tri = jnp.where(r < c, 1.0, 0.0).astype(BF16)
    ones = jnp.ones((LANES, LANES), BF16)
    return tri, ones


def _select_tile(kt, t_b, m_b, carry, tri, ones):
    eq = kt == t_b
    eqb = jnp.where(eq, 1.0, 0.0).astype(BF16)
    rank = carry + jnp.dot(eqb, tri, preferred_element_type=F32)
    sel = jnp.where(kt > t_b, 1.0, jnp.where(eq, jnp.where(rank < m_b, 1.0, 0.0), 0.0))
    return sel, carry + jnp.dot(eqb, ones, preferred_element_type=F32)


def _dsa_prompt_kernel(qi_ref, ikw_ref, qs_ref, kit_ref, kst_ref, vs_ref, b3_ref,
                       o_ref, keys_sc, msk_sc, m_sc, l_sc, acc_sc, *, k_sel, ch):
    b = pl.program_id(0)
    rows = o_ref.shape[0]
    sub = ch // LANES
    rt = rows // LANES
    n_ch = ((b + 1) * rows + ch - 1) // ch
    row_pos = b * rows + lax.broadcasted_iota(I32, (rows, 1), 0)

    w = ikw_ref[:, IDX_DIM:IDX_DIM + IDX_HEADS]

    def score_chunk(c, _):
        off = pl.multiple_of(c * ch, ch)
        kt = kit_ref[:, pl.ds(off, ch)]
        s = jnp.zeros((rows, ch), F32)
        for h in range(IDX_HEADS):
            d = jnp.dot(qi_ref[h], kt, preferred_element_type=F32)
            s = s + w[:, h:h + 1] * jnp.maximum(d, 0.0)
        col_pos = off + lax.broadcasted_iota(I32, (1, ch), 1)
        s = jnp.where(col_pos <= row_pos, s, NEG_INF)
        keys_sc[:, pl.ds(off, ch)] = _sort_key(s)
        return 0

    lax.fori_loop(0, n_ch, score_chunk, 0)

    t_b, m_b = _topk_threshold(keys_sc, rows, n_ch, ch, k_sel)

    m_sc[...] = jnp.full(m_sc.shape, NEG_INF, F32)
    l_sc[...] = jnp.zeros(l_sc.shape, F32)
    acc_sc[...] = jnp.zeros(acc_sc.shape, F32)
    tri, ones = _tie_constants()
    first_near = rt * b - 1

    def attend_chunk(c, carry, near):
        off = pl.multiple_of(c * ch, ch)
        for j in range(sub):
            kt = keys_sc[:, pl.ds(off + j * LANES, LANES)]
            sel, carry = _select_tile(kt, t_b, m_b, carry, tri, ones)
            if near:
                col_pos = off + j * LANES + lax.broadcasted_iota(I32, (1, LANES), 1)
                sel = jnp.where(col_pos <= row_pos, sel, 0.0)
            msk_sc[:, j * LANES:(j + 1) * LANES] = jnp.where(sel > 0.5, 0.0, NEG_INF)
        kst = kst_ref[:, pl.ds(off, ch)]
        vch = vs_ref[pl.ds(off, ch), :]
        for h in range(DSA_HEADS):
            s = jnp.dot(qs_ref[:, h * DSA_HD:(h + 1) * DSA_HD], kst, preferred_element_type=F32)
            if near:
                parts = []
                for j in range(sub):
                    rel = c * sub + j - first_near
                    bias = jnp.zeros((rows, LANES), F32)
                    for k in range(rt + 1):
                        bias = jnp.where(rel == k, b3_ref[h, k], bias)
                    parts.append(s[:, j * LANES:(j + 1) * LANES] + bias)
                s = jnp.concatenate(parts, axis=1)
            s = s + msk_sc[...]
            m_prev = m_sc[h]
            m_new = jnp.maximum(m_prev, jnp.max(s, axis=-1, keepdims=True))
            alpha = jnp.exp(m_prev - m_new)
            p = jnp.exp(s - jnp.tile(m_new, (1, sub)))
            l_sc[h] = alpha * l_sc[h] + jnp.sum(p, axis=-1, keepdims=True)
            acc_sc[h] = alpha * acc_sc[h] + jnp.dot(p.astype(BF16), vch, preferred_element_type=F32)
            m_sc[h] = m_new
        return carry

    c_near = (jnp.maximum(first_near, 0) * LANES) // ch
    carry = lax.fori_loop(0, c_near, lambda c, cr: attend_chunk(c, cr, False),
                          jnp.zeros((rows, LANES), F32))
    lax.fori_loop(c_near, n_ch, lambda c, cr: attend_chunk(c, cr, True), carry)
    for h in range(DSA_HEADS):
        o_ref[:, h * DSA_HD:(h + 1) * DSA_HD] = (acc_sc[h] / l_sc[h]).astype(o_ref.dtype)


def _dsa_prompt(qi4, ikw, qs, kit, kst, vsb, bias3, k_sel, rows, ch):
    t = qs.shape[0]
    assert DSA_HD == LANES and t % ch == 0 and ch % rows == 0
    full = lambda a: pl.BlockSpec(a.shape, lambda i: (0,) * a.ndim)
    kern = functools.partial(_dsa_prompt_kernel, k_sel=k_sel, ch=ch)
    return pl.pallas_call(
        kern,
        grid=(t // rows,),
        in_specs=[
            pl.BlockSpec((IDX_HEADS, rows, IDX_DIM), lambda i: (0, i, 0)),
            pl.BlockSpec((rows, LANES), lambda i: (i, 0)),
            pl.BlockSpec((rows, B_WIDTH), lambda i: (i, 0)),
            full(kit), full(kst), full(vsb), full(bias3),
        ],
        out_specs=pl.BlockSpec((rows, B_WIDTH), lambda i: (i, 0)),
        out_shape=jax.ShapeDtypeStruct((t, B_WIDTH), BF16),
        scratch_shapes=[pltpu.VMEM((rows, t), I32), pltpu.VMEM((rows, ch), F32),
                        pltpu.VMEM((DSA_HEADS, rows, LANES), F32),
                        pltpu.VMEM((DSA_HEADS, rows, LANES), F32),
                        pltpu.VMEM((DSA_HEADS, rows, DSA_HD), F32)],
        compiler_params=_cparams(("parallel",)),
        name="dsa_prompt",
    )(qi4, ikw, qs, kit, kst, vsb, bias3)


def _sample_diff_idx_kernel(pt_ref, *refs, lam_init, g_pages):
    n = g_pages
    k_refs = refs[0:n]
    v_refs = refs[n:2 * n]
    ik_refs = refs[2 * n:3 * n]
    (qd_ref, kd_ref, vd_ref, qib_ref, qif_ref, ki_ref, wi_ref, bl_ref, bn_ref, lamp_ref,
     sub_ref) = refs[3 * n:3 * n + 11]
    o_ref, sc_ref, sn_ref = refs[3 * n + 11:3 * n + 14]
    m_sc, l_sc, acc_sc = refs[3 * n + 14:]
    g = pl.program_id(1)
    ng = pl.num_programs(1)
    nrow = 2 * DIFF_HEADS

    @pl.when(g == 0)
    def _():
        m_sc[...] = jnp.full(m_sc.shape, NEG_INF, F32)
        l_sc[...] = jnp.zeros(l_sc.shape, F32)
        acc_sc[...] = jnp.zeros(acc_sc.shape, F32)

    lane_grp = lax.broadcasted_iota(I32, (nrow, DQ_W), 1) // DIFF_QK
    row_id = lax.broadcasted_iota(I32, (nrow, DQ_W), 0)
    qf = jnp.where(lane_grp == row_id, jnp.broadcast_to(qd_ref[0].astype(F32), (nrow, DQ_W)), 0.0)
    qbd = qf.astype(BF16)

    logits = []
    for j in range(n):
        s = jnp.dot(qbd, k_refs[j][0].astype(BF16), preferred_element_type=F32)
        if j == n - 1:
            s = s + jnp.where(g == ng - 1, bl_ref[...], 0.0)
        logits.append(s)
        d = jnp.dot(qib_ref[0], ik_refs[j][0].astype(BF16), preferred_element_type=F32)
        srow = jnp.sum(wi_ref[0] * jnp.maximum(d, 0.0), axis=0, keepdims=True)
        off = pl.multiple_of((g * n + j) * LANES, LANES)
        sc_ref[0, :, pl.ds(off, LANES)] = srow
    s = jnp.concatenate(logits, axis=1)
    m_prev = m_sc[...]
    m_new = jnp.maximum(m_prev, jnp.max(s, axis=-1, keepdims=True))
    alpha = jnp.exp(m_prev - m_new)
    p = jnp.exp(s - m_new)
    l_sc[...] = alpha * l_sc[...] + jnp.sum(p, axis=-1, keepdims=True)
    pv = []
    for h in range(DIFF_HEADS):
        acc_h = jnp.zeros((2, DIFF_V), F32)
        for j in range(n):
            vh = v_refs[j][0, pl.ds(h, LANES, stride=DIFF_HEADS), :].astype(BF16)
            ph = p[2 * h:2 * h + 2, j * LANES:(j + 1) * LANES].astype(BF16)
            acc_h = acc_h + jnp.dot(ph, vh, preferred_element_type=F32)
        pv.append(acc_h)
    acc_sc[...] = alpha * acc_sc[...] + jnp.concatenate(pv, axis=0)
    m_sc[...] = m_new

    @pl.when(g == ng - 1)
    def _():
        s_new = jnp.sum(qf * kd_ref[0], axis=-1, keepdims=True) + bn_ref[:, 0:1]
        m_prev = m_sc[...]
        m_new = jnp.maximum(m_prev, s_new)
        alpha = jnp.exp(m_prev - m_new)
        p_new = jnp.exp(s_new - m_new)
        l_fin = alpha * l_sc[...] + p_new
        acc = (alpha * acc_sc[...] + p_new * vd_ref[0]) / l_fin
        lam = _diff_lambda(lamp_ref[...], lam_init)
        for h in range(DIFF_HEADS):
            o = acc[2 * h:2 * h + 1] - lam * acc[2 * h + 1:2 * h + 2]
            o_ref[0, :, h * DIFF_V:(h + 1) * DIFF_V] = (
                _rms(o, sub_ref[...]) * (1.0 - lam_init)).astype(o_ref.dtype)
        d_new = jnp.sum(qif_ref[0] * ki_ref[0], axis=-1, keepdims=True)
        s_idx = jnp.sum(wi_ref[0][:, 0:1] * jnp.maximum(d_new, 0.0), axis=0, keepdims=True)
        sn_ref[0] = jnp.broadcast_to(s_idx, (1, LANES))


def _sample_diff_idx(page_table, ck, cv, cik, qd, kd, vd, qib, qif, ki, wi_b, bias_last, bias_new,
                     lam_p, subln, lam_init, g_pages):
    db, n_pages = page_table.shape
    page = ck.shape[2]
    past = n_pages * page
    n = g_pages
    ng = n_pages // n

    def page_spec(a, j):
        return pl.BlockSpec((1,) + a.shape[1:], lambda s, g, pt: (pt[s, g * n + j], 0, 0))

    per_s = lambda a: pl.BlockSpec((1,) + a.shape[1:], lambda s, g, pt: (s,) + (0,) * (a.ndim - 1))
    full = lambda a: pl.BlockSpec(a.shape, lambda s, g, pt: (0,) * a.ndim)
    in_specs = ([page_spec(ck, j) for j in range(n)]
                + [page_spec(cv, j) for j in range(n)]
                + [page_spec(cik, j) for j in range(n)]
                + [per_s(qd), per_s(kd), per_s(vd), per_s(qib), per_s(qif), per_s(ki), per_s(wi_b),
                   full(bias_last), full(bias_new), full(lam_p), full(subln)])
    kern = functools.partial(_sample_diff_idx_kernel, lam_init=lam_init, g_pages=n)
    nrow = 2 * DIFF_HEADS
    grid_spec = pltpu.PrefetchScalarGridSpec(
        num_scalar_prefetch=1,
        grid=(db, ng),
        in_specs=in_specs,
        out_specs=[pl.BlockSpec((1, 1, A_WIDTH), lambda s, g, pt: (s, 0, 0)),
                   pl.BlockSpec((1, 1, past), lambda s, g, pt: (s, 0, 0)),
                   pl.BlockSpec((1, 1, LANES), lambda s, g, pt: (s, 0, 0))],
        scratch_shapes=[pltpu.VMEM((nrow, 1), F32), pltpu.VMEM((nrow, 1), F32),
                        pltpu.VMEM((nrow, DIFF_V), F32)],
    )
    return pl.pallas_call(
        kern,
        grid_spec=grid_spec,
        out_shape=[jax.ShapeDtypeStruct((db, 1, A_WIDTH), BF16),
                   jax.ShapeDtypeStruct((db, 1, past), F32),
                   jax.ShapeDtypeStruct((db, 1, LANES), F32)],
        compiler_params=_cparams(("parallel", "arbitrary")),
        name="sample_diff_idx",
    )(page_table, *([ck] * n), *([cv] * n), *([cik] * n), qd, kd, vd, qib, qif, ki, wi_b,
      bias_last, bias_new, lam_p, subln)


def _sample_select_kernel(s_ref, o_ref, keys_sc, *, k_sel):
    rows, nk = s_ref.shape
    n_tiles = nk // LANES

    def to_keys(j, _):
        off = pl.multiple_of(j * LANES, LANES)
        keys_sc[:, pl.ds(off, LANES)] = _sort_key(s_ref[:, pl.ds(off, LANES)])
        return 0

    lax.fori_loop(0, n_tiles, to_keys, 0)
    sub = max(d for d in range(1, 9) if n_tiles % d == 0)
    t_b, m_b = _topk_threshold(keys_sc, rows, n_tiles // sub, sub * LANES, k_sel)
    tri, ones = _tie_constants()

    def emit(j, carry):
        off = pl.multiple_of(j * LANES, LANES)
        sel, carry = _select_tile(keys_sc[:, pl.ds(off, LANES)], t_b, m_b, carry, tri, ones)
        o_ref[:, pl.ds(off, LANES)] = jnp.where(sel > 0.5, 0.0, NEG_INF)
        return carry

    lax.fori_loop(0, n_tiles, emit, jnp.zeros((rows, LANES), F32))


def _sample_select(scores, k_sel):
    rows, nk = scores.shape
    return pl.pallas_call(
        functools.partial(_sample_select_kernel, k_sel=k_sel),
        out_shape=jax.ShapeDtypeStruct((rows, nk), F32),
        scratch_shapes=[pltpu.VMEM((rows, nk), I32)],
        compiler_params=pltpu.CompilerParams(vmem_limit_bytes=VMEM_LIMIT),
        name="sample_select",
    )(scores)


def _sample_dsa_kernel(pt_ref, *refs, g_pages):
    n = g_pages
    k_refs = refs[0:n]
    v_refs = refs[n:2 * n]
    qs_ref, ks_ref, vs_ref, msk_ref, bl_ref, bn_ref = refs[2 * n:2 * n + 6]
    o_ref = refs[2 * n + 6]
    m_sc, l_sc, acc_sc = refs[2 * n + 7:]
    g = pl.program_id(1)
    ng = pl.num_programs(1)

    @pl.when(g == 0)
    def _():
        m_sc[...] = jnp.full(m_sc.shape, NEG_INF, F32)
        l_sc[...] = jnp.zeros(l_sc.shape, F32)
        acc_sc[...] = jnp.zeros(acc_sc.shape, F32)

    q = qs_ref[0]
    logits = []
    for j in range(n):
        s = lax.dot_general(q, k_refs[j][0].astype(BF16), (((1,), (1,)), ((), ())),
                            preferred_element_type=F32)
        if j == n - 1:
            s = s + jnp.where(g == ng - 1, bl_ref[...], 0.0)
        off = pl.multiple_of((g * n + j) * LANES, LANES)
        logits.append(s + msk_ref[0, :, pl.ds(off, LANES)])
    s = jnp.concatenate(logits, axis=1)
    m_prev = m_sc[...]
    m_new = jnp.maximum(m_prev, jnp.max(s, axis=-1, keepdims=True))
    alpha = jnp.exp(m_prev - m_new)
    p = jnp.exp(s - m_new)
    l_sc[...] = alpha * l_sc[...] + jnp.sum(p, axis=-1, keepdims=True)
    pv = jnp.zeros(acc_sc.shape, F32)
    for j in range(n):
        pv = pv + jnp.dot(p[:, j * LANES:(j + 1) * LANES].astype(BF16), v_refs[j][0].astype(BF16),
                          preferred_element_type=F32)
    acc_sc[...] = alpha * acc_sc[...] + pv
    m_sc[...] = m_new

    @pl.when(g == ng - 1)
    def _():
        past = ng * n * LANES
        s_new = (jnp.sum(q.astype(F32) * ks_ref[0], axis=-1, keepdims=True) + bn_ref[:, 0:1]
                 + msk_ref[0, :, past:past + 1])
        m_prev = m_sc[...]
        m_new = jnp.maximum(m_prev, s_new)
        alpha = jnp.exp(m_prev - m_new)
        p_new = jnp.exp(s_new - m_new)
        l_fin = alpha * l_sc[...] + p_new
        acc = alpha * acc_sc[...] + p_new * vs_ref[0]
        o_ref[0] = (acc / l_fin).astype(o_ref.dtype)


def _sample_dsa(page_table, ck, cv, qs, ks, vs, mask, bias_last, bias_new, g_pages):
    db, n_pages = page_table.shape
    page = ck.shape[1]
    n = g_pages
    ng = n_pages // n

    def page_spec(width, j):
        return pl.BlockSpec((1, page, width), lambda s, g, pt: (pt[s, g * n + j], 0, 0))

    per_s = lambda a: pl.BlockSpec((1,) + a.shape[1:], lambda s, g, pt: (s,) + (0,) * (a.ndim - 1))
    full = lambda a: pl.BlockSpec(a.shape, lambda s, g, pt: (0,) * a.ndim)
    in_specs = ([page_spec(ck.shape[2], j) for j in range(n)]
                + [page_spec(cv.shape[2], j) for j in range(n)]
                + [per_s(qs), per_s(ks), per_s(vs), per_s(mask), full(bias_last), full(bias_new)])
    grid_spec = pltpu.PrefetchScalarGridSpec(
        num_scalar_prefetch=1,
        grid=(db, ng),
        in_specs=in_specs,
        out_specs=pl.BlockSpec((1, DSA_HEADS, DSA_HD), lambda s, g, pt: (s, 0, 0)),
        scratch_shapes=[pltpu.VMEM((DSA_HEADS, 1), F32), pltpu.VMEM((DSA_HEADS, 1), F32),
                        pltpu.VMEM((DSA_HEADS, DSA_HD), F32)],
    )
    return pl.pallas_call(
        functools.partial(_sample_dsa_kernel, g_pages=n),
        grid_spec=grid_spec,
        out_shape=jax.ShapeDtypeStruct((db, DSA_HEADS, DSA_HD), BF16),
        compiler_params=_cparams(("parallel", "arbitrary")),
        name="sample_dsa",
    )(page_table, *([ck] * n), *([cv] * n), qs, ks, vs, mask, bias_last, bias_new)


def _tail_a_kernel(x_ref, da_ref, sa_ref, gate_ref, wa_ref, wb_ref, wo_ref, n2_ref, x1_ref, h2_ref):
    d = x_ref.shape[1]
    ya = jnp.dot(da_ref[...], wa_ref[...], preferred_element_type=F32)
    yb = jnp.dot(sa_ref[...], wb_ref[...], preferred_element_type=F32)
    merged = gate_ref[:, :d].astype(F32) * ya + gate_ref[:, d:].astype(F32) * yb
    x1 = x_ref[...] + jnp.dot(merged.astype(BF16), wo_ref[...], preferred_element_type=F32)
    x1_ref[...] = x1
    h2_ref[...] = _rms(x1, n2_ref[...]).astype(BF16)


def _tail_a(x, da, sa, gate, wa, wb, wo, n2):
    t, d = x.shape
    tb = min(256, t)
    row = lambda n: pl.BlockSpec((tb, n), lambda i: (i, 0))
    full = lambda a: pl.BlockSpec(a.shape, lambda i: (0,) * a.ndim)
    return pl.pallas_call(
        _tail_a_kernel,
        grid=(t // tb,),
        in_specs=[row(d), row(da.shape[1]), row(sa.shape[1]), row(2 * d), full(wa), full(wb),
                  full(wo), full(n2)],
        out_specs=[row(d), row(d)],
        out_shape=[jax.ShapeDtypeStruct((t, d), F32), jax.ShapeDtypeStruct((t, d), BF16)],
        compiler_params=_cparams(("parallel",)),
        name="tail_a",
    )(x, da, sa, gate, wa, wb, wo, n2)


def _top_rows(work, n_top, store):
    rows = work.shape[0]
    iota = lax.broadcasted_iota(I32, work.shape, 0).astype(F32)
    mx = None
    for r in range(n_top):
        mx = jnp.max(work, axis=0, keepdims=True)
        store(r, mx)
        first = jnp.min(jnp.where(work == mx, iota, float(rows)), axis=0, keepdims=True)
        work = jnp.where(iota == first, -jnp.inf, work)
    return mx


def _staircase_pairs():
    return [(i, j) for i in range(PEER_TOPK) for j in range(PEER_TOPK // (i + 1))]


def _peer_route_kernel(h_ref, wq_ref, keys_ref, s1_ref, s2_ref, a_ref, b_ref, th_ref,
                       q_sc, top_sc, cand_sc):
    tb = h_ref.shape[0]
    q_sc[...] = jnp.dot(h_ref[...], wq_ref[...], preferred_element_type=F32).astype(BF16)
    pairs = _staircase_pairs()
    n_cand = cand_sc.shape[0]
    cand_sc[len(pairs):, :] = jnp.full((n_cand - len(pairs), tb), -jnp.inf, F32)
    s_refs = (s1_ref, s2_ref)

    def head(h, _):
        expo = []
        for c in range(2):
            off = pl.multiple_of((h * 2 + c) * PEER_KH, PEER_KH)
            s = lax.dot_general(keys_ref[c, h], q_sc[:, pl.ds(off, PEER_KH)],
                                (((1,), (1,)), ((), ())), preferred_element_type=F32)
            s_refs[c][h] = s

            def store(r, row, c=c):
                top_sc[c, r:r + 1, :] = row

            _top_rows(s, PEER_TOPK, store)
            expo.append(jnp.exp(s - top_sc[c, 0:1, :]))
        for r, (i, j) in enumerate(pairs):
            cand_sc[r:r + 1, :] = top_sc[0, i:i + 1, :] + top_sc[1, j:j + 1, :]
        cs = cand_sc[...]
        theta = _top_rows(cs, PEER_TOPK, lambda r, row: None)
        top_sum = top_sc[0, 0:1, :] + top_sc[1, 0:1, :]
        z = jnp.sum(jnp.where(cs >= theta, jnp.exp(cs - top_sum), 0.0), axis=0, keepdims=True)
        a_ref[h] = expo[0]
        b_ref[h] = expo[1] / z
        th_ref[pl.ds(h, 1), :] = theta
        return 0

    lax.fori_loop(0, PEER_HEADS, head, 0)


def _peer_route(h2, wq, keys):
    t, d = h2.shape
    tb = min(256, t)
    full = lambda a: pl.BlockSpec(a.shape, lambda i: (0,) * a.ndim)
    big = pl.BlockSpec((PEER_HEADS, PEER_NKEYS, tb), lambda i: (0, 0, i))
    big_shape = jax.ShapeDtypeStruct((PEER_HEADS, PEER_NKEYS, t), F32)
    n_cand = -(-len(_staircase_pairs()) // 8) * 8
    return pl.pallas_call(
        _peer_route_kernel,
        grid=(t // tb,),
        in_specs=[pl.BlockSpec((tb, d), lambda i: (i, 0)), full(wq), full(keys)],
        out_specs=[big, big, big, big, pl.BlockSpec((PEER_HEADS, tb), lambda i: (0, i))],
        out_shape=[big_shape, big_shape, big_shape, big_shape,
                   jax.ShapeDtypeStruct((PEER_HEADS, t), F32)],
        scratch_shapes=[pltpu.VMEM((tb, wq.shape[1]), BF16), pltpu.VMEM((2, PEER_TOPK, tb), F32),
                        pltpu.VMEM((n_cand, tb), F32)],
        compiler_params=_cparams(("parallel",)),
        name="peer_route",
    )(h2, wq, keys)


def _gelu(x):
    return 0.5 * x * (1.0 + lax.erf(x * math.sqrt(0.5)))


def _peer_dense_kernel(h_ref, eu_ref, evt_ref, s1_ref, s2_ref, a_ref, b_ref, th_ref, o_ref,
                       act_sc, p_sc):
    e = pl.program_id(1)
    ec = eu_ref.shape[0]
    n_i1 = ec // PEER_NKEYS

    @pl.when(e == 0)
    def _():
        o_ref[...] = jnp.zeros(o_ref.shape, F32)

    tb = h_ref.shape[0]
    tw = min(tb, 2 * LANES)
    act_sc[...] = _gelu(lax.dot_general(eu_ref[...], h_ref[...], (((1,), (1,)), ((), ())),
                                        preferred_element_type=F32))
    def tile(k, carry):
        jj = k // (tb // tw)
        t0 = pl.multiple_of((k % (tb // tw)) * tw, tw)
        i1 = e * n_i1 + jj
        rows = pl.ds(pl.multiple_of(jj * PEER_NKEYS, PEER_NKEYS), PEER_NKEYS)
        cols = pl.ds(t0, tw)
        w = jnp.zeros((PEER_NKEYS, tw), F32)
        for h in range(PEER_HEADS):
            ssum = s1_ref[h, pl.ds(i1, 1), cols] + s2_ref[h, :, cols]
            w = w + jnp.where(ssum >= th_ref[pl.ds(h, 1), cols],
                              a_ref[h, pl.ds(i1, 1), cols] * b_ref[h, :, cols], 0.0)
        p_sc[rows, cols] = (w * act_sc[rows, cols]).astype(BF16)
        return carry

    lax.fori_loop(0, n_i1 * (tb // tw), tile, 0)
    o_ref[...] += jnp.dot(evt_ref[...], p_sc[...], preferred_element_type=F32)


def _peer_dense(h2, eu, evt, s1, s2, a, b, theta, tb, ec):
    t, d = h2.shape
    n_exp = eu.shape[0]
    big = pl.BlockSpec((PEER_HEADS, PEER_NKEYS, tb), lambda i, e: (0, 0, i))
    return pl.pallas_call(
        _peer_dense_kernel,
        grid=(t // tb, n_exp // ec),
        in_specs=[pl.BlockSpec((tb, d), lambda i, e: (i, 0)),
                  pl.BlockSpec((ec, d), lambda i, e: (e, 0)),
                  pl.BlockSpec((d, ec), lambda i, e: (0, e)),
                  big, big, big, big,
                  pl.BlockSpec((PEER_HEADS, tb), lambda i, e: (0, i))],
        out_specs=pl.BlockSpec((d, tb), lambda i, e: (0, i)),
        out_shape=jax.ShapeDtypeStruct((d, t), F32),
        scratch_shapes=[pltpu.VMEM((ec, tb), F32), pltpu.VMEM((ec, tb), BF16)],
        compiler_params=_cparams(("parallel", "arbitrary")),
        name="peer_dense",
    )(h2, eu, evt, s1, s2, a, b, theta)


def _tail_b_kernel(x1_ref, ff_ref, p_ref, pg_ref, pp_ref, y_ref):
    x2 = x1_ref[...] + ff_ref[...]
    gate = jax.nn.sigmoid(jnp.dot(x2.astype(BF16), pg_ref[...], preferred_element_type=F32))
    y_ref[...] = x2 + gate * jnp.dot(p_ref[...].astype(BF16), pp_ref[...], preferred_element_type=F32)


def _tail_b(x1, ff, p, pg, pp):
    t, d = x1.shape
    tb = min(256, t)
    row = lambda n: pl.BlockSpec((tb, n), lambda i: (i, 0))
    full = lambda a: pl.BlockSpec(a.shape, lambda i: (0,) * a.ndim)
    return pl.pallas_call(
        _tail_b_kernel,
        grid=(t // tb,),
        in_specs=[row(d), row(d), row(p.shape[1]), full(pg), full(pp)],
        out_specs=row(d),
        out_shape=jax.ShapeDtypeStruct((t, d), F32),
        compiler_params=_cparams(("parallel",)),
        name="tail_b",
    )(x1, ff, p, pg, pp)


def _layer_tail(x, da, sa, gate, p, w):
    x1, h2 = _tail_a(x, da, sa, gate, w["wa"], w["wb"], w["wo"], w["n2"])
    s1, s2, a, b, theta = _peer_route(h2, w["wq"], w["keys"])
    t = x.shape[0]
    tb = PEER_TB if t % PEER_TB == 0 else min(LANES, t)
    fft = _peer_dense(h2, w["eu"], w["evt"], s1, s2, a, b, theta, tb, PEER_EC)
    return _tail_b(x1, fft.T, p, w["pg"], w["pp"])


def _bias_tiles128(tab):
    d = np.arange(LANES)[:, None] - np.arange(LANES)[None, :]
    tab = tab - tab[N_BUCKETS - 1:N_BUCKETS]
    keep = jnp.asarray(d >= 0)[None]
    low = jnp.where(keep, jnp.moveaxis(tab[_bucket_of(np.maximum(d, 0))], -1, 0), 0.0)
    up = jnp.where(keep, 0.0, jnp.moveaxis(tab[_bucket_of(np.maximum(d + LANES, 0))], -1, 0))
    return low, up


def _block_bias(low, up, rows, cols, shift):
    zero = jnp.zeros_like(low)
    out = []
    for a in range(rows // LANES):
        row = []
        for b in range(cols // LANES):
            k = shift // LANES + a - b
            row.append(low if k == 0 else up if k == 1 else zero)
        out.append(jnp.concatenate(row, axis=2))
    return jnp.concatenate(out, axis=1)


def _decode_bias(tab, page, rep):
    tab = tab - tab[N_BUCKETS - 1:N_BUCKETS]
    last = jnp.repeat(tab[_bucket_of(page - np.arange(page))].T, rep, axis=0)
    new = jnp.repeat(jnp.broadcast_to(tab[0][:, None], (tab.shape[1], LANES)), rep, axis=0)
    return last, new


def _tile_lanes(g, reps):
    return jnp.tile(g.reshape(1, -1), (1, reps))


def _group_mean_matrix(width, group):
    idx = np.arange(width) // group
    return jnp.asarray((idx[:, None] == idx[None, :]).astype(np.float32) / group, BF16)


def kernel(x_prompt, x_sample, p_prompt, p_sample, cache_diff_k, cache_diff_v, cache_dsa_k, cache_dsa_v, cache_idx_k, page_table, rel_bias, norm1_g, w_in, diff_q_norm, diff_k_norm, diff_lambda_p, diff_subln, dsa_q_norm, dsa_k_norm, w_branch_a, w_branch_b, w_out, norm2_g, peer_wq, peer_keys, peer_u, peer_v, ple_gate, ple_proj):
    depth = w_in.shape[0]
    batch, seq, d_model = x_prompt.shape
    db, dec_seq, _ = x_sample.shape
    assert batch == 1 and dec_seq == 1
    n_pages = page_table.shape[1]
    page = cache_diff_k.shape[2]
    past = n_pages * page
    assert page == LANES and seq % DSA_CHUNK == 0
    k_sel_p = min(TOPK_MAX, seq // 4)
    k_sel_s = min(TOPK_MAX, (past + dec_seq) // 4)
    g_pages = 8 if n_pages % 8 == 0 else 1

    tab_d = rel_bias[:, :DIFF_HEADS]
    tab_s = rel_bias[:, DIFF_HEADS:]
    low_d, up_d = _bias_tiles128(tab_d)
    bias_dd = _block_bias(low_d, up_d, DIFF_BLK, DIFF_BLK, 0)
    bias_dp = _block_bias(low_d, up_d, DIFF_BLK, DIFF_BLK, DIFF_BLK)
    low_s, up_s = _bias_tiles128(tab_s)
    bias3 = jnp.stack([_block_bias(low_s, up_s, DSA_ROWS, LANES, LANES) if k == 0 else
                       jnp.concatenate([jnp.zeros((DSA_HEADS, (k - 1) * LANES, LANES), F32),
                                        _block_bias(low_s, up_s, DSA_ROWS - (k - 1) * LANES, LANES, 0)],
                                       axis=1)
                       for k in range(DSA_ROWS // LANES + 1)], axis=1)
    dec_d_last, dec_d_new = _decode_bias(tab_d, page, 2)
    dec_s_last, dec_s_new = _decode_bias(tab_s, page, 1)
    g64 = _group_mean_matrix(DQ_W, DIFF_QK)
    g128 = _group_mean_matrix(B_WIDTH, DSA_HD)

    xp = x_prompt.reshape(seq, d_model)
    xs = x_sample.reshape(db, d_model)
    outs_p = {k: [] for k in ("dk", "dv", "sk", "sv", "ik")}
    outs_s = {k: [] for k in ("dk", "dv", "sk", "sv", "ik")}
    for l in range(depth):
        lam_init = 0.8 - 0.6 * math.exp(-0.3 * l)
        n_split = OFF_IKW + IKW_USED
        w_pad = jnp.concatenate(
            [w_in[l][:, :n_split], jnp.zeros((d_model, LANES - IKW_USED), F32), w_in[l][:, n_split:]],
            axis=1).astype(BF16)
        g1 = norm1_g[l].reshape(1, d_model)
        dqn = _tile_lanes(diff_q_norm[l], DQ_W // DIFF_QK)
        dkn = _tile_lanes(diff_k_norm[l], DQ_W // DIFF_QK)
        sqn = _tile_lanes(dsa_q_norm[l], DSA_HEADS)
        skn = dsa_k_norm[l].reshape(1, DSA_HD)
        lam_p = diff_lambda_p[l]
        subln = diff_subln[l].reshape(1, DIFF_V)
        tail_w = dict(
            wa=w_branch_a[l].astype(BF16), wb=w_branch_b[l].astype(BF16), wo=w_out[l].astype(BF16),
            n2=norm2_g[l].reshape(1, d_model), wq=peer_wq[l].astype(BF16),
            keys=peer_keys[l].astype(BF16), eu=peer_u[l].astype(BF16),
            evt=peer_v[l].T.astype(BF16), pg=ple_gate[l].astype(BF16), pp=ple_proj[l].astype(BF16))

        (qd, kd, kdb, vd, vdb, qs, ks, ksb, vs, vsb, qi, ikw, kib, gate) = _proj(
            xp, g1, w_pad, dqn, dkn, sqn, skn, g64, g128)
        q4 = qd.reshape(seq, DIFF_HEADS, 2, DIFF_QK).transpose(1, 2, 0, 3)
        kt4 = kdb.reshape(seq, DIFF_HEADS, 2, DIFF_QK).transpose(1, 2, 3, 0)
        v3 = vdb.reshape(seq, DIFF_HEADS, DIFF_V).transpose(1, 0, 2)
        da = _diff_attn(q4, kt4, v3, bias_dd, bias_dp, lam_p, subln, lam_init, DIFF_BLK)
        qi4 = qi.reshape(seq, IDX_HEADS, IDX_DIM).transpose(1, 0, 2)
        sa = _dsa_prompt(qi4, ikw, qs, kib[:, :IDX_DIM].T, ksb.T, vsb, bias3, k_sel_p,
                         DSA_ROWS, DSA_CHUNK)
        xp = _layer_tail(xp, da, sa, gate, p_prompt[l, 0], tail_w)
        outs_p["dk"].append(kd.reshape(batch, seq, DIFF_HEADS, 2, DIFF_QK))
        outs_p["dv"].append(vd.reshape(batch, seq, DIFF_HEADS, DIFF_V))
        outs_p["sk"].append(ks.reshape(batch, seq, DSA_HD))
        outs_p["sv"].append(vs.reshape(batch, seq, DSA_HD))
        outs_p["ik"].append(ikw[:, :IDX_DIM].reshape(batch, seq, IDX_DIM))

        (qd2, kd2, _, vd2, _, qs2, ks2, _, vs2, _, qi2, ikw2, _, gate2) = _proj(
            xs, g1, w_pad, dqn, dkn, sqn, skn, g64, g128)
        ki2 = ikw2[:, :IDX_DIM]
        wi2 = ikw2[:, IDX_DIM:IDX_DIM + IDX_HEADS]
        n_phys = cache_diff_k.shape[1]
        da_s, sc_past, sc_new = _sample_diff_idx(
            page_table,
            jnp.transpose(cache_diff_k[l], (0, 2, 3, 4, 1)).reshape(n_phys, DQ_W, page),
            cache_diff_v[l].reshape(n_phys, page * DIFF_HEADS, DIFF_V),
            jnp.transpose(cache_idx_k[l], (0, 2, 1)),
            qd2.reshape(db, 1, DQ_W), kd2.reshape(db, 1, DQ_W),
            jnp.repeat(vd2.reshape(db, DIFF_HEADS, DIFF_V), 2, axis=1),
            qi2.reshape(db, IDX_HEADS, IDX_DIM), qi2.astype(F32).reshape(db, IDX_HEADS, IDX_DIM),
            ki2.reshape(db, 1, IDX_DIM),
            jnp.broadcast_to(wi2[:, :, None], (db, IDX_HEADS, LANES)),
            dec_d_last, dec_d_new, lam_p, subln, lam_init, g_pages)
        scores = jnp.concatenate(
            [sc_past.reshape(db, past), sc_new.reshape(db, LANES)[:, :1],
             jnp.full((db, LANES - 1), -jnp.inf, F32)], axis=1)
        mask = _sample_select(scores, k_sel_s)
        sa_s = _sample_dsa(
            page_table, cache_dsa_k[l], cache_dsa_v[l],
            qs2.reshape(db, DSA_HEADS, DSA_HD), ks2.reshape(db, 1, DSA_HD), vs2.reshape(db, 1, DSA_HD),
            mask.reshape(db, 1, past + LANES), dec_s_last, dec_s_new, g_pages)
        xs = _layer_tail(xs, da_s.reshape(db, A_WIDTH), sa_s.reshape(db, B_WIDTH), gate2,
                         p_sample[l, :, 0], tail_w)
        outs_s["dk"].append(kd2.reshape(db, dec_seq, DIFF_HEADS, 2, DIFF_QK))
        outs_s["dv"].append(vd2.reshape(db, dec_seq, DIFF_HEADS, DIFF_V))
        outs_s["sk"].append(ks2.reshape(db, dec_seq, DSA_HD))
        outs_s["sv"].append(vs2.reshape(db, dec_seq, DSA_HD))
        outs_s["ik"].append(ki2.reshape(db, dec_seq, IDX_DIM))

    st = jnp.stack
    return (xp.reshape(batch, seq, d_model), xs.reshape(db, dec_seq, d_model),
            st(outs_p["dk"]), st(outs_p["dv"]), st(outs_p["sk"]), st(outs_p["sv"]), st(outs_p["ik"]),
            st(outs_s["dk"]), st(outs_s["dv"]), st(outs_s["sk"]), st(outs_s["sv"]), st(outs_s["ik"]))
```

```python
import functools
import math

import numpy as np
import jax
import jax.numpy as jnp
from jax import lax
from jax.experimental import pallas as pl
from jax.experimental.pallas import tpu as pltpu

F32 = jnp.float32
BF16 = jnp.bfloat16
I32 = jnp.int32

DIFF_HEADS = 4
DIFF_QK = 64
DIFF_V = 128
DSA_HEADS = 4
DSA_HD = 128
IDX_HEADS = 4
IDX_DIM = 64
TOPK_MAX = 256
N_BUCKETS = 32
MAX_EXACT = 16
MAX_DIST = 128
PEER_HEADS = 8
PEER_NKEYS = 128
PEER_KH = 128
PEER_TOPK = 16
RMS_EPS = 1e-6
NEG_INF = -1e30

LANES = 128
VMEM_LIMIT = 56 * 1024 * 1024

DIFF_BLK = 512
DSA_ROWS = 256
DSA_CHUNK = 1024
PEER_TB = 512
PEER_EC = 512

A_WIDTH = DIFF_HEADS * DIFF_V
B_WIDTH = DSA_HEADS * DSA_HD
DQ_W = DIFF_HEADS * 2 * DIFF_QK
IQ_W = IDX_HEADS * IDX_DIM

OFF_DQ = 0
OFF_DK = OFF_DQ + DQ_W
OFF_DV = OFF_DK + DQ_W
OFF_SQ = OFF_DV + A_WIDTH
OFF_SK = OFF_SQ + B_WIDTH
OFF_SV = OFF_SK + DSA_HD
OFF_IQ = OFF_SV + DSA_HD
OFF_IKW = OFF_IQ + IQ_W
OFF_G = OFF_IKW + LANES
IKW_USED = IDX_DIM + IDX_HEADS


def _bucket_table():
    d = np.arange(MAX_DIST)
    nf = np.maximum(d, 1).astype(np.float64)
    val = np.log(nf / MAX_EXACT) / math.log(MAX_DIST / MAX_EXACT) * (N_BUCKETS - MAX_EXACT)
    frac = np.abs(val - np.round(val))
    assert np.all(frac[MAX_EXACT + 1:] > 1e-4)
    large = MAX_EXACT + np.floor(val + 1e-9).astype(np.int64)
    return np.where(d < MAX_EXACT, d, np.minimum(large, N_BUCKETS - 1)).astype(np.int32)


_BUCKET = _bucket_table()


def _bucket_of(dist):
    dist = np.asarray(dist)
    return np.where(dist < MAX_DIST, _BUCKET[np.clip(dist, 0, MAX_DIST - 1)], N_BUCKETS - 1)


def _cparams(sem):
    return pltpu.CompilerParams(dimension_semantics=sem, vmem_limit_bytes=VMEM_LIMIT)


def _rms(x, gain):
    return x * lax.rsqrt(jnp.mean(x * x, axis=-1, keepdims=True) + RMS_EPS) * gain


def _proj_kernel(x_ref, g1_ref, w_ref, dqn_ref, dkn_ref, sqn_ref, skn_ref, g64_ref, g128_ref,
                 qd_ref, kd_ref, kdb_ref, vd_ref, vdb_ref, qs_ref, ks_ref, ksb_ref,
                 vs_ref, vsb_ref, qi_ref, ikw_ref, kib_ref, gate_ref):
    hb = _rms(x_ref[...], g1_ref[...]).astype(BF16)

    def seg(off, n):
        return jnp.dot(hb, w_ref[:, off:off + n], preferred_element_type=F32)

    def group_norm(z, gmat, gain):
        sq = z * z
        hi = sq.astype(BF16)
        lo = (sq - hi.astype(F32)).astype(BF16)
        ms = (jnp.dot(hi, gmat, preferred_element_type=F32)
              + jnp.dot(lo, gmat, preferred_element_type=F32))
        return z * lax.rsqrt(ms + RMS_EPS) * gain

    g64 = g64_ref[...]
    g128 = g128_ref[...]
    qd = group_norm(seg(OFF_DQ, DQ_W), g64, dqn_ref[...])
    qd_ref[...] = (qd * (DIFF_QK ** -0.5)).astype(BF16)
    kd = group_norm(seg(OFF_DK, DQ_W), g64, dkn_ref[...])
    kd_ref[...] = kd
    kdb_ref[...] = kd.astype(BF16)
    vd = seg(OFF_DV, A_WIDTH)
    vd_ref[...] = vd
    vdb_ref[...] = vd.astype(BF16)
    qs = group_norm(seg(OFF_SQ, B_WIDTH), g128, sqn_ref[...])
    qs_ref[...] = (qs * (DSA_HD ** -0.5)).astype(BF16)
    ks = _rms(seg(OFF_SK, DSA_HD), skn_ref[...])
    ks_ref[...] = ks
    ksb_ref[...] = ks.astype(BF16)
    vs = seg(OFF_SV, DSA_HD)
    vs_ref[...] = vs
    vsb_ref[...] = vs.astype(BF16)
    qi_ref[...] = seg(OFF_IQ, IQ_W).astype(BF16)
    ikw = seg(OFF_IKW, LANES)
    ikw_ref[...] = ikw
    kib_ref[...] = ikw.astype(BF16)
    d_model = x_ref.shape[1]
    for half in range(2):
        zg = seg(OFF_G + half * d_model, d_model)
        gate_ref[:, half * d_model:(half + 1) * d_model] = jax.nn.sigmoid(zg).astype(BF16)


def _proj(x, g1, w_pad, dqn, dkn, sqn, skn, g64, g128):
    t, d = x.shape
    tb = min(256, t)
    n_in = w_pad.shape[1]
    row = lambda n: pl.BlockSpec((tb, n), lambda i: (i, 0))
    full = lambda a: pl.BlockSpec(a.shape, lambda i: (0,) * a.ndim)
    outs = [
        (DQ_W, BF16),
        (DQ_W, F32), (DQ_W, BF16),
        (A_WIDTH, F32), (A_WIDTH, BF16),
        (B_WIDTH, BF16),
        (DSA_HD, F32), (DSA_HD, BF16),
        (DSA_HD, F32), (DSA_HD, BF16),
        (IQ_W, BF16),
        (LANES, F32), (LANES, BF16),
        (2 * d, BF16),
    ]
    return pl.pallas_call(
        _proj_kernel,
        grid=(t // tb,),
        in_specs=[row(d), full(g1), full(w_pad), full(dqn), full(dkn), full(sqn), full(skn),
                  full(g64), full(g128)],
        out_specs=[row(n) for n, _ in outs],
        out_shape=[jax.ShapeDtypeStruct((t, n), dt) for n, dt in outs],
        compiler_params=_cparams(("parallel",)),
        name="proj",
    )(x, g1, w_pad, dqn, dkn, sqn, skn, g64, g128)


def _diff_lambda(lp, lam_init):
    e1 = jnp.exp(jnp.sum(lp[0:1] * lp[1:2], axis=-1, keepdims=True))
    e2 = jnp.exp(jnp.sum(lp[2:3] * lp[3:4], axis=-1, keepdims=True))
    return e1 - e2 + lam_init


def _diff_attn_kernel(q_ref, kt_ref, v_ref, bd_ref, bp_ref, lamp_ref, sub_ref, o_ref,
                      m_sc, l_sc, acc_sc, *, lam_init):
    qi = pl.program_id(1)
    blk = q_ref.shape[2]
    n_rep = blk // LANES
    m_sc[...] = jnp.full(m_sc.shape, NEG_INF, F32)
    l_sc[...] = jnp.zeros(l_sc.shape, F32)
    acc_sc[...] = jnp.zeros(acc_sc.shape, F32)

    def chunk(j, kind):
        off = pl.multiple_of(j * blk, blk)
        v = v_ref[0, pl.ds(off, blk), :]
        for c in range(2):
            s = jnp.dot(q_ref[0, c], kt_ref[0, c, :, pl.ds(off, blk)], preferred_element_type=F32)
            if kind == 1:
                s = s + bp_ref[0]
            if kind == 2:
                r = lax.broadcasted_iota(I32, (blk, blk), 0)
                cc = lax.broadcasted_iota(I32, (blk, blk), 1)
                s = jnp.where(r >= cc, s + bd_ref[0], NEG_INF)
            m_prev = m_sc[c]
            m_new = jnp.maximum(m_prev, jnp.max(s, axis=-1, keepdims=True))
            alpha = jnp.exp(m_prev - m_new)
            p = jnp.exp(s - jnp.tile(m_new, (1, n_rep)))
            l_sc[c] = alpha * l_sc[c] + jnp.sum(p, axis=-1, keepdims=True)
            acc_sc[c] = alpha * acc_sc[c] + jnp.dot(p.astype(BF16), v, preferred_element_type=F32)
            m_sc[c] = m_new

    def far(j, carry):
        chunk(j, 0)
        return carry

    lax.fori_loop(0, jnp.maximum(qi - 1, 0), far, 0)
    pl.when(qi >= 1)(lambda: chunk(qi - 1, 1))
    chunk(qi, 2)
    lam = _diff_lambda(lamp_ref[...], lam_init)
    o = acc_sc[0] / l_sc[0] - lam * (acc_sc[1] / l_sc[1])
    o_ref[...] = (_rms(o, sub_ref[...]) * (1.0 - lam_init)).astype(o_ref.dtype)


def _diff_attn(q4, kt4, v3, bias_d, bias_p, lam_p, subln, lam_init, blk):
    h, _, t, dqk = q4.shape
    dv = v3.shape[2]
    assert dv == LANES
    kern = functools.partial(_diff_attn_kernel, lam_init=lam_init)
    return pl.pallas_call(
        kern,
        grid=(h, t // blk),
        in_specs=[
            pl.BlockSpec((1, 2, blk, dqk), lambda hh, qi: (hh, 0, qi, 0)),
            pl.BlockSpec((1, 2, dqk, t), lambda hh, qi: (hh, 0, 0, 0)),
            pl.BlockSpec((1, t, dv), lambda hh, qi: (hh, 0, 0)),
            pl.BlockSpec((1, blk, blk), lambda hh, qi: (hh, 0, 0)),
            pl.BlockSpec((1, blk, blk), lambda hh, qi: (hh, 0, 0)),
            pl.BlockSpec(lam_p.shape, lambda hh, qi: (0, 0)),
            pl.BlockSpec(subln.shape, lambda hh, qi: (0, 0)),
        ],
        out_specs=pl.BlockSpec((blk, dv), lambda hh, qi: (qi, hh)),
        out_shape=jax.ShapeDtypeStruct((t, h * dv), BF16),
        scratch_shapes=[pltpu.VMEM((2, blk, LANES), F32), pltpu.VMEM((2, blk, LANES), F32),
                        pltpu.VMEM((2, blk, dv), F32)],
        compiler_params=_cparams(("parallel", "arbitrary")),
        name="diff_attn",
    )(q4, kt4, v3, bias_d, bias_p, lam_p, subln)


def _sort_key(s):
    s = jnp.where(s == 0.0, 0.0, s)
    bits = pltpu.bitcast(s, I32)
    return jnp.where(bits < 0, bits ^ jnp.int32(0x7FFFFFFF), bits)


ROW_GROUP = 64


def _count_keys(keys_sc, r0, rg, n_ch, ch, pred):
    def body(c, acc):
        off = pl.multiple_of(c * ch, ch)
        for j in range(ch // LANES):
            kt = keys_sc[r0:r0 + rg, pl.ds(off + j * LANES, LANES)]
            acc = acc + jnp.where(pred(kt), 1.0, 0.0)
        return acc

    acc = lax.fori_loop(0, n_ch, body, jnp.zeros((rg, LANES), F32))
    return jnp.sum(acc, axis=-1, keepdims=True)


def _topk_threshold(keys_sc, rows, n_ch, ch, k_sel):
    ts, ms = [], []
    for r0 in range(0, rows, ROW_GROUP):
        rg = min(ROW_GROUP, rows - r0)

        def bis(state, r0=r0, rg=rg):
            i, t, done = state
            cand = t ^ jnp.left_shift(jnp.int32(1), 31 - i)
            cand_b = jnp.broadcast_to(cand, (rg, LANES))
            cnt = _count_keys(keys_sc, r0, rg, n_ch, ch, lambda kt: kt >= cand_b)
            t = jnp.where(done > 0.5, t, jnp.where(cnt >= float(k_sel), cand, t))
            return i + 1, t, jnp.where(cnt == float(k_sel), 1.0, done)

        def unfinished(state):
            i, _, done = state
            return jnp.logical_and(i < 32, jnp.min(done) < 0.5)

        _, t, _ = lax.while_loop(unfinished, bis, (jnp.int32(0), jnp.full((rg, 1), -2 ** 31, I32),
                                                   jnp.zeros((rg, 1), F32)))
        t_b = jnp.broadcast_to(t, (rg, LANES))
        c_gt = _count_keys(keys_sc, r0, rg, n_ch, ch, lambda kt: kt > t_b)
        ts.append(t_b)
        ms.append(jnp.broadcast_to(float(k_sel) - c_gt, (rg, LANES)))
    return jnp.concatenate(ts, axis=0), jnp.concatenate(ms, axis=0)


def _topk_threshold_short(keys_sc, rows, k_sel):
    n_tiles = keys_sc.shape[1] // LANES
    groups = [(r0, min(ROW_GROUP, rows - r0)) for r0 in range(0, rows, ROW_GROUP)]

    def count(r0, rg, pred):
        acc = jnp.zeros((rg, LANES), F32)
        for j in range(n_tiles):
            acc = acc + jnp.where(pred(keys_sc[r0:r0 + rg, j * LANES:(j + 1) * LANES]), 1.0, 0.0)
        return jnp.sum(acc, axis=-1, keepdims=True)

    def bis(i, ts):
        bit = jnp.left_shift(jnp.int32(1), 31 - i)
        out = []
        for (r0, rg), t in zip(groups, ts):
            cand_b = jnp.broadcast_to(t ^ bit, (rg, LANES))
            cnt = count(r0, rg, lambda kt: kt >= cand_b)
            out.append(jnp.where(cnt >= float(k_sel), t ^ bit, t))
        return tuple(out)

    ts = lax.fori_loop(0, 32, bis, tuple(jnp.full((rg, 1), -2 ** 31, I32) for _, rg in groups))
    t_bs, m_bs = [], []
    for (r0, rg), t in zip(groups, ts):
        t_b = jnp.broadcast_to(t, (rg, LANES))
        c_gt = count(r0, rg, lambda kt: kt > t_b)
        t_bs.append(t_b)
        m_bs.append(jnp.broadcast_to(float(k_sel) - c_gt, (rg, LANES)))
    return jnp.concatenate(t_bs, axis=0), jnp.concatenate(m_bs, axis=0)


CAND_DEPTH = 12
SUBLANES = 8


def _lane_top_keys(keys_sc, cand_sc, rows, n_ch, ch):
    lowest = jnp.full((SUBLANES, LANES), -jnp.inf, F32)

    def flip(k):
        return jnp.where(k < 0, k ^ jnp.int32(0x7FFFFFFF), k)

    def group(g, _):
        r0 = pl.multiple_of(g * SUBLANES, SUBLANES)

        def chunk(c, regs):
            off = pl.multiple_of(c * ch, ch)
            for j in range(ch // LANES):
                x = pltpu.bitcast(flip(keys_sc[pl.ds(r0, SUBLANES), pl.ds(off + j * LANES, LANES)]), F32)
                new = []
                for reg in regs:
                    new.append(jnp.maximum(reg, x))
                    x = jnp.minimum(reg, x)
                regs = tuple(new)
            return regs

        regs = lax.fori_loop(0, n_ch, chunk, (lowest,) * CAND_DEPTH)
        for i, reg in enumerate(regs):
            cand_sc[pl.ds(r0, SUBLANES), i * LANES:(i + 1) * LANES] = flip(pltpu.bitcast(reg, I32))
        return 0

    lax.fori_loop(0, rows // SUBLANES, group, 0)


def _tie_constants():
    r = lax.broadcasted_iota(I32, (LANES, LANES), 0)
    c = lax.broadcasted_iota(I32, (LANES, LANES), 1)
    tri = jnp.where(r < c, 1.0, 0.0).astype(BF16)
    ones = jnp.ones((LANES, LANES), BF16)
    return tri, ones


def _select_tile(kt, t_b, m_b, carry, tri, ones):
    eq = kt == t_b
    eqb = jnp.where(eq, 1.0, 0.0).astype(BF16)
    rank = carry + jnp.dot(eqb, tri, preferred_element_type=F32)
    sel = jnp.where(kt > t_b, 1.0, jnp.where(eq, jnp.where(rank < m_b, 1.0, 0.0), 0.0))
    return sel, carry + jnp.dot(eqb, ones, preferred_element_type=F32)


def _dsa_prompt_kernel(qi_ref, ikw_ref, qs_ref, kit_ref, kst_ref, vs_ref, b3_ref,
                       o_ref, keys_sc, cand_sc, msk_sc, m_sc, l_sc, acc_sc, *, k_sel, ch):
    b = pl.program_id(0)
    rows = o_ref.shape[0]
    sub = ch // LANES
    rt = rows // LANES
    n_ch = ((b + 1) * rows + ch - 1) // ch
    row_pos = b * rows + lax.broadcasted_iota(I32, (rows, 1), 0)

    w = ikw_ref[:, IDX_DIM:IDX_DIM + IDX_HEADS]

    def score_chunk(c, _):
        off = pl.multiple_of(c * ch, ch)
        kt = kit_ref[:, pl.ds(off, ch)]
        s = jnp.zeros((rows, ch), F32)
        for h in range(IDX_HEADS):
            d = jnp.dot(qi_ref[h], kt, preferred_element_type=F32)
            s = s + w[:, h:h + 1] * jnp.maximum(d, 0.0)
        col_pos = off + lax.broadcasted_iota(I32, (1, ch), 1)
        s = jnp.where(col_pos <= row_pos, s, NEG_INF)
        keys_sc[:, pl.ds(off, ch)] = _sort_key(s)
        return 0

    lax.fori_loop(0, n_ch, score_chunk, 0)

    _lane_top_keys(keys_sc, cand_sc, rows, n_ch, ch)
    t_b, m_b = _topk_threshold_short(cand_sc, rows, k_sel)
    dropped_above = jnp.max(jnp.where(cand_sc[:, (CAND_DEPTH - 1) * LANES:] > t_b, 1.0, 0.0))
    t_b, m_b = lax.cond(dropped_above < 0.5, lambda: (t_b, m_b),
                        lambda: _topk_threshold(keys_sc, rows, n_ch, ch, k_sel))

    m_sc[...] = jnp.full(m_sc.shape, NEG_INF, F32)
    l_sc[...] = jnp.zeros(l_sc.shape, F32)
    acc_sc[...] = jnp.zeros(acc_sc.shape, F32)
    tri, ones = _tie_constants()
    first_near = rt * b - 1

    def attend_chunk(c, carry, near):
        off = pl.multiple_of(c * ch, ch)
        for j in range(sub):
            kt = keys_sc[:, pl.ds(off + j * LANES, LANES)]
            sel, carry = _select_tile(kt, t_b, m_b, carry, tri, ones)
            if near:
                col_pos = off + j * LANES + lax.broadcasted_iota(I32, (1, LANES), 1)
                sel = jnp.where(col_pos <= row_pos, sel, 0.0)
            msk_sc[:, j * LANES:(j + 1) * LANES] = jnp.where(sel > 0.5, 0.0, NEG_INF)
        kst = kst_ref[:, pl.ds(off, ch)]
        vch = vs_ref[pl.ds(off, ch), :]
        for h in range(DSA_HEADS):
            s = jnp.dot(qs_ref[:, h * DSA_HD:(h + 1) * DSA_HD], kst, preferred_element_type=F32)
            if near:
                parts = []
                for j in range(sub):
                    rel = c * sub + j - first_near
                    bias = jnp.zeros((rows, LANES), F32)
                    for k in range(rt + 1):
                        bias = jnp.where(rel == k, b3_ref[h, k], bias)
                    parts.append(s[:, j * LANES:(j + 1) * LANES] + bias)
                s = jnp.concatenate(parts, axis=1)
            s = s + msk_sc[...]
            m_prev = m_sc[h]
            m_new = jnp.maximum(m_prev, jnp.max(s, axis=-1, keepdims=True))
            alpha = jnp.exp(m_prev - m_new)
            p = jnp.exp(s - jnp.tile(m_new, (1, sub)))
            l_sc[h] = alpha * l_sc[h] + jnp.sum(p, axis=-1, keepdims=True)
            acc_sc[h] = alpha * acc_sc[h] + jnp.dot(p.astype(BF16), vch, preferred_element_type=F32)
            m_sc[h] = m_new
        return carry

    c_near = (jnp.maximum(first_near, 0) * LANES) // ch
    carry = lax.fori_loop(0, c_near, lambda c, cr: attend_chunk(c, cr, False),
                          jnp.zeros((rows, LANES), F32))
    lax.fori_loop(c_near, n_ch, lambda c, cr: attend_chunk(c, cr, True), carry)
    for h in range(DSA_HEADS):
        o_ref[:, h * DSA_HD:(h + 1) * DSA_HD] = (acc_sc[h] / l_sc[h]).astype(o_ref.dtype)


def _dsa_prompt(qi4, ikw, qs, kit, kst, vsb, bias3, k_sel, rows, ch):
    t = qs.shape[0]
    assert DSA_HD == LANES and t % ch == 0 and ch % rows == 0
    full = lambda a: pl.BlockSpec(a.shape, lambda i: (0,) * a.ndim)
    kern = functools.partial(_dsa_prompt_kernel, k_sel=k_sel, ch=ch)
    return pl.pallas_call(
        kern,
        grid=(t // rows,),
        in_specs=[
            pl.BlockSpec((IDX_HEADS, rows, IDX_DIM), lambda i: (0, i, 0)),
            pl.BlockSpec((rows, LANES), lambda i: (i, 0)),
            pl.BlockSpec((rows, B_WIDTH), lambda i: (i, 0)),
            full(kit), full(kst), full(vsb), full(bias3),
        ],
        out_specs=pl.BlockSpec((rows, B_WIDTH), lambda i: (i, 0)),
        out_shape=jax.ShapeDtypeStruct((t, B_WIDTH), BF16),
        scratch_shapes=[pltpu.VMEM((rows, t), I32), pltpu.VMEM((rows, CAND_DEPTH * LANES), I32),
                        pltpu.VMEM((rows, ch), F32),
                        pltpu.VMEM((DSA_HEADS, rows, LANES), F32),
                        pltpu.VMEM((DSA_HEADS, rows, LANES), F32),
                        pltpu.VMEM((DSA_HEADS, rows, DSA_HD), F32)],
        compiler_params=_cparams(("parallel",)),
        name="dsa_prompt",
    )(qi4, ikw, qs, kit, kst, vsb, bias3)


def _sample_diff_idx_kernel(pt_ref, *refs, lam_init, g_pages):
    n = g_pages
    k_refs = refs[0:n]
    v_refs = refs[n:2 * n]
    ik_refs = refs[2 * n:3 * n]
    (qd_ref, kd_ref, vd_ref, qib_ref, qif_ref, ki_ref, wi_ref, bl_ref, bn_ref, lamp_ref,
     sub_ref) = refs[3 * n:3 * n + 11]
    o_ref, sc_ref, sn_ref = refs[3 * n + 11:3 * n + 14]
    m_sc, l_sc, acc_sc = refs[3 * n + 14:]
    g = pl.program_id(1)
    ng = pl.num_programs(1)
    nrow = 2 * DIFF_HEADS

    @pl.when(g == 0)
    def _():
        m_sc[...] = jnp.full(m_sc.shape, NEG_INF, F32)
        l_sc[...] = jnp.zeros(l_sc.shape, F32)
        acc_sc[...] = jnp.zeros(acc_sc.shape, F32)

    lane_grp = lax.broadcasted_iota(I32, (nrow, DQ_W), 1) // DIFF_QK
    row_id = lax.broadcasted_iota(I32, (nrow, DQ_W), 0)
    qf = jnp.where(lane_grp == row_id, jnp.broadcast_to(qd_ref[0].astype(F32), (nrow, DQ_W)), 0.0)
    qbd = qf.astype(BF16)

    logits = []
    for j in range(n):
        s = jnp.dot(qbd, k_refs[j][0].astype(BF16), preferred_element_type=F32)
        if j == n - 1:
            s = s + jnp.where(g == ng - 1, bl_ref[...], 0.0)
        logits.append(s)
        d = jnp.dot(qib_ref[0], ik_refs[j][0].astype(BF16), preferred_element_type=F32)
        srow = jnp.sum(wi_ref[0] * jnp.maximum(d, 0.0), axis=0, keepdims=True)
        off = pl.multiple_of((g * n + j) * LANES, LANES)
        sc_ref[0, :, pl.ds(off, LANES)] = srow
    s = jnp.concatenate(logits, axis=1)
    m_prev = m_sc[...]
    m_new = jnp.maximum(m_prev, jnp.max(s, axis=-1, keepdims=True))
    alpha = jnp.exp(m_prev - m_new)
    p = jnp.exp(s - m_new)
    l_sc[...] = alpha * l_sc[...] + jnp.sum(p, axis=-1, keepdims=True)
    pv = []
    for h in range(DIFF_HEADS):
        acc_h = jnp.zeros((2, DIFF_V), F32)
        for j in range(n):
            vh = v_refs[j][0, pl.ds(h, LANES, stride=DIFF_HEADS), :].astype(BF16)
            ph = p[2 * h:2 * h + 2, j * LANES:(j + 1) * LANES].astype(BF16)
            acc_h = acc_h + jnp.dot(ph, vh, preferred_element_type=F32)
        pv.append(acc_h)
    acc_sc[...] = alpha * acc_sc[...] + jnp.concatenate(pv, axis=0)
    m_sc[...] = m_new

    @pl.when(g == ng - 1)
    def _():
        s_new = jnp.sum(qf * kd_ref[0], axis=-1, keepdims=True) + bn_ref[:, 0:1]
        m_prev = m_sc[...]
        m_new = jnp.maximum(m_prev, s_new)
        alpha = jnp.exp(m_prev - m_new)
        p_new = jnp.exp(s_new - m_new)
        l_fin = alpha * l_sc[...] + p_new
        acc = (alpha * acc_sc[...] + p_new * vd_ref[0]) / l_fin
        lam = _diff_lambda(lamp_ref[...], lam_init)
        for h in range(DIFF_HEADS):
            o = acc[2 * h:2 * h + 1] - lam * acc[2 * h + 1:2 * h + 2]
            o_ref[0, :, h * DIFF_V:(h + 1) * DIFF_V] = (
                _rms(o, sub_ref[...]) * (1.0 - lam_init)).astype(o_ref.dtype)
        d_new = jnp.sum(qif_ref[0] * ki_ref[0], axis=-1, keepdims=True)
        s_idx = jnp.sum(wi_ref[0][:, 0:1] * jnp.maximum(d_new, 0.0), axis=0, keepdims=True)
        sn_ref[0] = jnp.broadcast_to(s_idx, (1, LANES))


def _sample_diff_idx(page_table, ck, cv, cik, qd, kd, vd, qib, qif, ki, wi_b, bias_last, bias_new,
                     lam_p, subln, lam_init, g_pages):
    db, n_pages = page_table.shape
    page = ck.shape[2]
    past = n_pages * page
    n = g_pages
    ng = n_pages // n

    def page_spec(a, j):
        return pl.BlockSpec((1,) + a.shape[1:], lambda s, g, pt: (pt[s, g * n + j], 0, 0))

    per_s = lambda a: pl.BlockSpec((1,) + a.shape[1:], lambda s, g, pt: (s,) + (0,) * (a.ndim - 1))
    full = lambda a: pl.BlockSpec(a.shape, lambda s, g, pt: (0,) * a.ndim)
    in_specs = ([page_spec(ck, j) for j in range(n)]
                + [page_spec(cv, j) for j in range(n)]
                + [page_spec(cik, j) for j in range(n)]
                + [per_s(qd), per_s(kd), per_s(vd), per_s(qib), per_s(qif), per_s(ki), per_s(wi_b),
                   full(bias_last), full(bias_new), full(lam_p), full(subln)])
    kern = functools.partial(_sample_diff_idx_kernel, lam_init=lam_init, g_pages=n)
    nrow = 2 * DIFF_HEADS
    grid_spec = pltpu.PrefetchScalarGridSpec(
        num_scalar_prefetch=1,
        grid=(db, ng),
        in_specs=in_specs,
        out_specs=[pl.BlockSpec((1, 1, A_WIDTH), lambda s, g, pt: (s, 0, 0)),
                   pl.BlockSpec((1, 1, past), lambda s, g, pt: (s, 0, 0)),
                   pl.BlockSpec((1, 1, LANES), lambda s, g, pt: (s, 0, 0))],
        scratch_shapes=[pltpu.VMEM((nrow, 1), F32), pltpu.VMEM((nrow, 1), F32),
                        pltpu.VMEM((nrow, DIFF_V), F32)],
    )
    return pl.pallas_call(
        kern,
        grid_spec=grid_spec,
        out_shape=[jax.ShapeDtypeStruct((db, 1, A_WIDTH), BF16),
                   jax.ShapeDtypeStruct((db, 1, past), F32),
                   jax.ShapeDtypeStruct((db, 1, LANES), F32)],
        compiler_params=_cparams(("parallel", "arbitrary")),
        name="sample_diff_idx",
    )(page_table, *([ck] * n), *([cv] * n), *([cik] * n), qd, kd, vd, qib, qif, ki, wi_b,
      bias_last, bias_new, lam_p, subln)


def _sample_select_kernel(s_ref, o_ref, keys_sc, *, k_sel):
    rows, nk = s_ref.shape
    n_tiles = nk // LANES

    def to_keys(j, _):
        off = pl.multiple_of(j * LANES, LANES)
        keys_sc[:, pl.ds(off, LANES)] = _sort_key(s_ref[:, pl.ds(off, LANES)])
        return 0

    lax.fori_loop(0, n_tiles, to_keys, 0)
    sub = max(d for d in range(1, 9) if n_tiles % d == 0)
    t_b, m_b = _topk_threshold(keys_sc, rows, n_tiles // sub, sub * LANES, k_sel)
    tri, ones = _tie_constants()

    def emit(j, carry):
        off = pl.multiple_of(j * LANES, LANES)
        sel, carry = _select_tile(keys_sc[:, pl.ds(off, LANES)], t_b, m_b, carry, tri, ones)
        o_ref[:, pl.ds(off, LANES)] = jnp.where(sel > 0.5, 0.0, NEG_INF)
        return carry

    lax.fori_loop(0, n_tiles, emit, jnp.zeros((rows, LANES), F32))


def _sample_select(scores, k_sel):
    rows, nk = scores.shape
    return pl.pallas_call(
        functools.partial(_sample_select_kernel, k_sel=k_sel),
        out_shape=jax.ShapeDtypeStruct((rows, nk), F32),
        scratch_shapes=[pltpu.VMEM((rows, nk), I32)],
        compiler_params=pltpu.CompilerParams(vmem_limit_bytes=VMEM_LIMIT),
        name="sample_select",
    )(scores)


def _sample_dsa_kernel(pt_ref, *refs, g_pages):
    n = g_pages
    k_refs = refs[0:n]
    v_refs = refs[n:2 * n]
    qs_ref, ks_ref, vs_ref, msk_ref, bl_ref, bn_ref = refs[2 * n:2 * n + 6]
    o_ref = refs[2 * n + 6]
    m_sc, l_sc, acc_sc = refs[2 * n + 7:]
    g = pl.program_id(1)
    ng = pl.num_programs(1)

    @pl.when(g == 0)
    def _():
        m_sc[...] = jnp.full(m_sc.shape, NEG_INF, F32)
        l_sc[...] = jnp.zeros(l_sc.shape, F32)
        acc_sc[...] = jnp.zeros(acc_sc.shape, F32)

    q = qs_ref[0]
    logits = []
    for j in range(n):
        s = lax.dot_general(q, k_refs[j][0].astype(BF16), (((1,), (1,)), ((), ())),
                            preferred_element_type=F32)
        if j == n - 1:
            s = s + jnp.where(g == ng - 1, bl_ref[...], 0.0)
        off = pl.multiple_of((g * n + j) * LANES, LANES)
        logits.append(s + msk_ref[0, :, pl.ds(off, LANES)])
    s = jnp.concatenate(logits, axis=1)
    m_prev = m_sc[...]
    m_new = jnp.maximum(m_prev, jnp.max(s, axis=-1, keepdims=True))
    alpha = jnp.exp(m_prev - m_new)
    p = jnp.exp(s - m_new)
    l_sc[...] = alpha * l_sc[...] + jnp.sum(p, axis=-1, keepdims=True)
    pv = jnp.zeros(acc_sc.shape, F32)
    for j in range(n):
        pv = pv + jnp.dot(p[:, j * LANES:(j + 1) * LANES].astype(BF16), v_refs[j][0].astype(BF16),
                          preferred_element_type=F32)
    acc_sc[...] = alpha * acc_sc[...] + pv
    m_sc[...] = m_new

    @pl.when(g == ng - 1)
    def _():
        past = ng * n * LANES
        s_new = (jnp.sum(q.astype(F32) * ks_ref[0], axis=-1, keepdims=True) + bn_ref[:, 0:1]
                 + msk_ref[0, :, past:past + 1])
        m_prev = m_sc[...]
        m_new = jnp.maximum(m_prev, s_new)
        alpha = jnp.exp(m_prev - m_new)
        p_new = jnp.exp(s_new - m_new)
        l_fin = alpha * l_sc[...] + p_new
        acc = alpha * acc_sc[...] + p_new * vs_ref[0]
        o_ref[0] = (acc / l_fin).astype(o_ref.dtype)


def _sample_dsa(page_table, ck, cv, qs, ks, vs, mask, bias_last, bias_new, g_pages):
    db, n_pages = page_table.shape
    page = ck.shape[1]
    n = g_pages
    ng = n_pages // n

    def page_spec(width, j):
        return pl.BlockSpec((1, page, width), lambda s, g, pt: (pt[s, g * n + j], 0, 0))

    per_s = lambda a: pl.BlockSpec((1,) + a.shape[1:], lambda s, g, pt: (s,) + (0,) * (a.ndim - 1))
    full = lambda a: pl.BlockSpec(a.shape, lambda s, g, pt: (0,) * a.ndim)
    in_specs = ([page_spec(ck.shape[2], j) for j in range(n)]
                + [page_spec(cv.shape[2], j) for j in range(n)]
                + [per_s(qs), per_s(ks), per_s(vs), per_s(mask), full(bias_last), full(bias_new)])
    grid_spec = pltpu.PrefetchScalarGridSpec(
        num_scalar_prefetch=1,
        grid=(db, ng),
        in_specs=in_specs,
        out_specs=pl.BlockSpec((1, DSA_HEADS, DSA_HD), lambda s, g, pt: (s, 0, 0)),
        scratch_shapes=[pltpu.VMEM((DSA_HEADS, 1), F32), pltpu.VMEM((DSA_HEADS, 1), F32),
                        pltpu.VMEM((DSA_HEADS, DSA_HD), F32)],
    )
    return pl.pallas_call(
        functools.partial(_sample_dsa_kernel, g_pages=n),
        grid_spec=grid_spec,
        out_shape=jax.ShapeDtypeStruct((db, DSA_HEADS, DSA_HD), BF16),
        compiler_params=_cparams(("parallel", "arbitrary")),
        name="sample_dsa",
    )(page_table, *([ck] * n), *([cv] * n), qs, ks, vs, mask, bias_last, bias_new)


def _tail_a_kernel(x_ref, da_ref, sa_ref, gate_ref, wa_ref, wb_ref, wo_ref, n2_ref, x1_ref, h2_ref):
    d = x_ref.shape[1]
    ya = jnp.dot(da_ref[...], wa_ref[...], preferred_element_type=F32)
    yb = jnp.dot(sa_ref[...], wb_ref[...], preferred_element_type=F32)
    merged = gate_ref[:, :d].astype(F32) * ya + gate_ref[:, d:].astype(F32) * yb
    x1 = x_ref[...] + jnp.dot(merged.astype(BF16), wo_ref[...], preferred_element_type=F32)
    x1_ref[...] = x1
    h2_ref[...] = _rms(x1, n2_ref[...]).astype(BF16)


def _tail_a(x, da, sa, gate, wa, wb, wo, n2):
    t, d = x.shape
    tb = min(256, t)
    row = lambda n: pl.BlockSpec((tb, n), lambda i: (i, 0))
    full = lambda a: pl.BlockSpec(a.shape, lambda i: (0,) * a.ndim)
    return pl.pallas_call(
        _tail_a_kernel,
        grid=(t // tb,),
        in_specs=[row(d), row(da.shape[1]), row(sa.shape[1]), row(2 * d), full(wa), full(wb),
                  full(wo), full(n2)],
        out_specs=[row(d), row(d)],
        out_shape=[jax.ShapeDtypeStruct((t, d), F32), jax.ShapeDtypeStruct((t, d), BF16)],
        compiler_params=_cparams(("parallel",)),
        name="tail_a",
    )(x, da, sa, gate, wa, wb, wo, n2)


def _top_rows(work, n_top, store, want_rank=False):
    rows = work.shape[0]
    iota = lax.broadcasted_iota(I32, work.shape, 0).astype(F32)
    rank = jnp.full(work.shape, float(rows), F32) if want_rank else None
    mx = None
    for r in range(n_top):
        mx = jnp.max(work, axis=0, keepdims=True)
        store(r, mx)
        first = jnp.min(jnp.where(work == mx, iota, float(rows)), axis=0, keepdims=True)
        hit = iota == first
        work = jnp.where(hit, -jnp.inf, work)
        if want_rank:
            rank = jnp.where(hit, float(r), rank)
    return mx, rank


def _staircase_pairs():
    return [(i, j) for i in range(PEER_TOPK) for j in range(PEER_TOPK // (i + 1))]


def _peer_route_kernel(h_ref, wq_ref, keys_ref, a_ref, n_ref, b_ref, r2_ref, q_sc, top_sc, cand_sc):
    tb = h_ref.shape[0]
    q_sc[...] = jnp.dot(h_ref[...], wq_ref[...], preferred_element_type=F32).astype(BF16)
    pairs = _staircase_pairs()
    n_cand = cand_sc.shape[0]
    cand_sc[len(pairs):, :] = jnp.full((n_cand - len(pairs), tb), -jnp.inf, F32)

    def head(h, _):
        expo, ranks = [], []
        for c in range(2):
            off = pl.multiple_of((h * 2 + c) * PEER_KH, PEER_KH)
            s = lax.dot_general(keys_ref[c, h], q_sc[:, pl.ds(off, PEER_KH)],
                                (((1,), (1,)), ((), ())), preferred_element_type=F32)

            def store(r, row, c=c):
                top_sc[c, r:r + 1, :] = row

            _, rank = _top_rows(s, PEER_TOPK, store, want_rank=True)
            ranks.append(rank)
            expo.append(jnp.exp(s - top_sc[c, 0:1, :]))
        for r, (i, j) in enumerate(pairs):
            cand_sc[r:r + 1, :] = top_sc[0, i:i + 1, :] + top_sc[1, j:j + 1, :]
        cs = cand_sc[...]
        theta, _ = _top_rows(cs, PEER_TOPK, lambda r, row: None)
        top_sum = top_sc[0, 0:1, :] + top_sc[1, 0:1, :]
        keep = cs >= theta
        z = jnp.sum(jnp.where(keep, jnp.exp(cs - top_sum), 0.0), axis=0, keepdims=True)
        n = jnp.zeros((PEER_NKEYS, tb), F32)
        r0 = 0
        for i in range(PEER_TOPK):
            width = PEER_TOPK // (i + 1)
            n_i = jnp.sum(jnp.where(cand_sc[r0:r0 + width, :] >= theta, 1.0, 0.0),
                          axis=0, keepdims=True)
            n = jnp.where(ranks[0] == float(i), n_i, n)
            r0 += width
        a_ref[h] = expo[0]
        n_ref[h] = n
        b_ref[h] = (expo[1] / z).astype(BF16)
        r2_ref[h] = ranks[1].astype(BF16)
        return 0

    lax.fori_loop(0, PEER_HEADS, head, 0)


def _peer_route(h2, wq, keys):
    t, d = h2.shape
    tb = min(256, t)
    full = lambda a: pl.BlockSpec(a.shape, lambda i: (0,) * a.ndim)
    big = pl.BlockSpec((PEER_HEADS, PEER_NKEYS, tb), lambda i: (0, 0, i))
    shape = lambda dt: jax.ShapeDtypeStruct((PEER_HEADS, PEER_NKEYS, t), dt)
    n_cand = -(-len(_staircase_pairs()) // 8) * 8
    return pl.pallas_call(
        _peer_route_kernel,
        grid=(t // tb,),
        in_specs=[pl.BlockSpec((tb, d), lambda i: (i, 0)), full(wq), full(keys)],
        out_specs=[big, big, big, big],
        out_shape=[shape(F32), shape(F32), shape(BF16), shape(BF16)],
        scratch_shapes=[pltpu.VMEM((tb, wq.shape[1]), BF16), pltpu.VMEM((2, PEER_TOPK, tb), F32),
                        pltpu.VMEM((n_cand, tb), F32)],
        compiler_params=_cparams(("parallel",)),
        name="peer_route",
    )(h2, wq, keys)


BF16_SUBLANES = 16


def _bcast_rows_bf16(row, rows):
    one = jnp.broadcast_to(row, (BF16_SUBLANES, row.shape[1])).astype(BF16)
    return jnp.tile(one, (rows // BF16_SUBLANES, 1))


def _gelu(x):
    return 0.5 * x * (1.0 + lax.erf(x * math.sqrt(0.5)))


def _peer_dense_kernel(h_ref, eu_ref, evt_ref, a_ref, n_ref, b_ref, r2_ref, o_ref, w_sc, p_sc):
    e = pl.program_id(1)
    last = pl.num_programs(1) - 2
    ec = eu_ref.shape[0]
    n_i1 = ec // PEER_NKEYS
    tb = h_ref.shape[0]
    tw = min(tb, 2 * LANES)
    slot = e % 2

    @pl.when(e == 0)
    def _():
        o_ref[...] = jnp.zeros(o_ref.shape, F32)
        p_sc[1] = jnp.zeros(p_sc.shape[1:], BF16)

    o_ref[...] += jnp.dot(evt_ref[...], p_sc[1 - slot], preferred_element_type=F32)

    cur = jnp.minimum(e, last)
    for jj in range(n_i1):
        i1 = cur * n_i1 + jj
        rows = slice(jj * PEER_NKEYS, (jj + 1) * PEER_NKEYS)
        for t0 in range(0, tb, tw):
            cols = slice(t0, t0 + tw)
            w = jnp.zeros((PEER_NKEYS, tw), BF16)
            for h in range(PEER_HEADS):
                a_b = _bcast_rows_bf16(a_ref[h, pl.ds(i1, 1), cols], PEER_NKEYS)
                n_b = _bcast_rows_bf16(n_ref[h, pl.ds(i1, 1), cols], PEER_NKEYS)
                w = w + jnp.where(r2_ref[h, :, cols] < n_b, a_b * b_ref[h, :, cols],
                                  jnp.zeros((), BF16))
            w_sc[rows, cols] = w
    act = _gelu(lax.dot_general(eu_ref[...], h_ref[...], (((1,), (1,)), ((), ())),
                                preferred_element_type=F32))
    p_sc[slot] = w_sc[...] * act.astype(BF16)


def _peer_dense(h2, eu, evt, a, n, b, r2, tb, ec):
    t, d = h2.shape
    ne = eu.shape[0] // ec
    big = pl.BlockSpec((PEER_HEADS, PEER_NKEYS, tb), lambda i, e: (0, 0, i))
    return pl.pallas_call(
        _peer_dense_kernel,
        grid=(t // tb, ne + 1),
        in_specs=[pl.BlockSpec((tb, d), lambda i, e: (i, 0)),
                  pl.BlockSpec((ec, d), lambda i, e: (jnp.minimum(e, ne - 1), 0)),
                  pl.BlockSpec((d, ec), lambda i, e: (0, jnp.maximum(e - 1, 0))),
                  big, big, big, big],
        out_specs=pl.BlockSpec((d, tb), lambda i, e: (0, i)),
        out_shape=jax.ShapeDtypeStruct((d, t), F32),
        scratch_shapes=[pltpu.VMEM((ec, tb), BF16), pltpu.VMEM((2, ec, tb), BF16)],
        compiler_params=_cparams(("parallel", "arbitrary")),
        name="peer_dense",
    )(h2, eu, evt, a, n, b, r2)


def _tail_b_kernel(x1_ref, ff_ref, p_ref, pg_ref, pp_ref, y_ref):
    x2 = x1_ref[...] + ff_ref[...]
    gate = jax.nn.sigmoid(jnp.dot(x2.astype(BF16), pg_ref[...], preferred_element_type=F32))
    y_ref[...] = x2 + gate * jnp.dot(p_ref[...].astype(BF16), pp_ref[...], preferred_element_type=F32)


def _tail_b(x1, ff, p, pg, pp):
    t, d = x1.shape
    tb = min(256, t)
    row = lambda n: pl.BlockSpec((tb, n), lambda i: (i, 0))
    full = lambda a: pl.BlockSpec(a.shape, lambda i: (0,) * a.ndim)
    return pl.pallas_call(
        _tail_b_kernel,
        grid=(t // tb,),
        in_specs=[row(d), row(d), row(p.shape[1]), full(pg), full(pp)],
        out_specs=row(d),
        out_shape=jax.ShapeDtypeStruct((t, d), F32),
        compiler_params=_cparams(("parallel",)),
        name="tail_b",
    )(x1, ff, p, pg, pp)


def _layer_tail(x, da, sa, gate, p, w):
    x1, h2 = _tail_a(x, da, sa, gate, w["wa"], w["wb"], w["wo"], w["n2"])
    a, n, b, r2 = _peer_route(h2, w["wq"], w["keys"])
    t = x.shape[0]
    tb = PEER_TB if t % PEER_TB == 0 else min(LANES, t)
    fft = _peer_dense(h2, w["eu"], w["evt"], a, n, b, r2, tb, PEER_EC)
    return _tail_b(x1, fft.T, p, w["pg"], w["pp"])


def _bias_tiles128(tab):
    d = np.arange(LANES)[:, None] - np.arange(LANES)[None, :]
    tab = tab - tab[N_BUCKETS - 1:N_BUCKETS]
    keep = jnp.asarray(d >= 0)[None]
    low = jnp.where(keep, jnp.moveaxis(tab[_bucket_of(np.maximum(d, 0))], -1, 0), 0.0)
    up = jnp.where(keep, 0.0, jnp.moveaxis(tab[_bucket_of(np.maximum(d + LANES, 0))], -1, 0))
    return low, up


def _block_bias(low, up, rows, cols, shift):
    zero = jnp.zeros_like(low)
    out = []
    for a in range(rows // LANES):
        row = []
        for b in range(cols // LANES):
            k = shift // LANES + a - b
            row.append(low if k == 0 else up if k == 1 else zero)
        out.append(jnp.concatenate(row, axis=2))
    return jnp.concatenate(out, axis=1)


def _decode_bias(tab, page, rep):
    tab = tab - tab[N_BUCKETS - 1:N_BUCKETS]
    last = jnp.repeat(tab[_bucket_of(page - np.arange(page))].T, rep, axis=0)
    new = jnp.repeat(jnp.broadcast_to(tab[0][:, None], (tab.shape[1], LANES)), rep, axis=0)
    return last, new


def _tile_lanes(g, reps):
    return jnp.tile(g.reshape(1, -1), (1, reps))


def _group_mean_matrix(width, group):
    idx = np.arange(width) // group
    return jnp.asarray((idx[:, None] == idx[None, :]).astype(np.float32) / group, BF16)


def kernel(x_prompt, x_sample, p_prompt, p_sample, cache_diff_k, cache_diff_v, cache_dsa_k, cache_dsa_v, cache_idx_k, page_table, rel_bias, norm1_g, w_in, diff_q_norm, diff_k_norm, diff_lambda_p, diff_subln, dsa_q_norm, dsa_k_norm, w_branch_a, w_branch_b, w_out, norm2_g, peer_wq, peer_keys, peer_u, peer_v, ple_gate, ple_proj):
    depth = w_in.shape[0]
    batch, seq, d_model = x_prompt.shape
    db, dec_seq, _ = x_sample.shape
    assert batch == 1 and dec_seq == 1
    n_pages = page_table.shape[1]
    page = cache_diff_k.shape[2]
    past = n_pages * page
    assert page == LANES and seq % DSA_CHUNK == 0
    k_sel_p = min(TOPK_MAX, seq // 4)
    k_sel_s = min(TOPK_MAX, (past + dec_seq) // 4)
    g_pages = 8 if n_pages % 8 == 0 else 1

    tab_d = rel_bias[:, :DIFF_HEADS]
    tab_s = rel_bias[:, DIFF_HEADS:]
    low_d, up_d = _bias_tiles128(tab_d)
    bias_dd = _block_bias(low_d, up_d, DIFF_BLK, DIFF_BLK, 0)
    bias_dp = _block_bias(low_d, up_d, DIFF_BLK, DIFF_BLK, DIFF_BLK)
    low_s, up_s = _bias_tiles128(tab_s)
    bias3 = jnp.stack([_block_bias(low_s, up_s, DSA_ROWS, LANES, LANES) if k == 0 else
                       jnp.concatenate([jnp.zeros((DSA_HEADS, (k - 1) * LANES, LANES), F32),
                                        _block_bias(low_s, up_s, DSA_ROWS - (k - 1) * LANES, LANES, 0)],
                                       axis=1)
                       for k in range(DSA_ROWS // LANES + 1)], axis=1)
    dec_d_last, dec_d_new = _decode_bias(tab_d, page, 2)
    dec_s_last, dec_s_new = _decode_bias(tab_s, page, 1)
    g64 = _group_mean_matrix(DQ_W, DIFF_QK)
    g128 = _group_mean_matrix(B_WIDTH, DSA_HD)

    xp = x_prompt.reshape(seq, d_model)
    xs = x_sample.reshape(db, d_model)
    outs_p = {k: [] for k in ("dk", "dv", "sk", "sv", "ik")}
    outs_s = {k: [] for k in ("dk", "dv", "sk", "sv", "ik")}
    for l in range(depth):
        lam_init = 0.8 - 0.6 * math.exp(-0.3 * l)
        n_split = OFF_IKW + IKW_USED
        w_pad = jnp.concatenate(
            [w_in[l][:, :n_split], jnp.zeros((d_model, LANES - IKW_USED), F32), w_in[l][:, n_split:]],
            axis=1).astype(BF16)
        g1 = norm1_g[l].reshape(1, d_model)
        dqn = _tile_lanes(diff_q_norm[l], DQ_W // DIFF_QK)
        dkn = _tile_lanes(diff_k_norm[l], DQ_W // DIFF_QK)
        sqn = _tile_lanes(dsa_q_norm[l], DSA_HEADS)
        skn = dsa_k_norm[l].reshape(1, DSA_HD)
        lam_p = diff_lambda_p[l]
        subln = diff_subln[l].reshape(1, DIFF_V)
        tail_w = dict(
            wa=w_branch_a[l].astype(BF16), wb=w_branch_b[l].astype(BF16), wo=w_out[l].astype(BF16),
            n2=norm2_g[l].reshape(1, d_model), wq=peer_wq[l].astype(BF16),
            keys=peer_keys[l].astype(BF16), eu=peer_u[l].astype(BF16),
            evt=peer_v[l].T.astype(BF16), pg=ple_gate[l].astype(BF16), pp=ple_proj[l].astype(BF16))

        (qd, kd, kdb, vd, vdb, qs, ks, ksb, vs, vsb, qi, ikw, kib, gate) = _proj(
            xp, g1, w_pad, dqn, dkn, sqn, skn, g64, g128)
        q4 = qd.reshape(seq, DIFF_HEADS, 2, DIFF_QK).transpose(1, 2, 0, 3)
        kt4 = kdb.reshape(seq, DIFF_HEADS, 2, DIFF_QK).transpose(1, 2, 3, 0)
        v3 = vdb.reshape(seq, DIFF_HEADS, DIFF_V).transpose(1, 0, 2)
        da = _diff_attn(q4, kt4, v3, bias_dd, bias_dp, lam_p, subln, lam_init, DIFF_BLK)
        qi4 = qi.reshape(seq, IDX_HEADS, IDX_DIM).transpose(1, 0, 2)
        sa = _dsa_prompt(qi4, ikw, qs, kib[:, :IDX_DIM].T, ksb.T, vsb, bias3, k_sel_p,
                         DSA_ROWS, DSA_CHUNK)
        xp = _layer_tail(xp, da, sa, gate, p_prompt[l, 0], tail_w)
        outs_p["dk"].append(kd.reshape(batch, seq, DIFF_HEADS, 2, DIFF_QK))
        outs_p["dv"].append(vd.reshape(batch, seq, DIFF_HEADS, DIFF_V))
        outs_p["sk"].append(ks.reshape(batch, seq, DSA_HD))
        outs_p["sv"].append(vs.reshape(batch, seq, DSA_HD))
        outs_p["ik"].append(ikw[:, :IDX_DIM].reshape(batch, seq, IDX_DIM))

        (qd2, kd2, _, vd2, _, qs2, ks2, _, vs2, _, qi2, ikw2, _, gate2) = _proj(
            xs, g1, w_pad, dqn, dkn, sqn, skn, g64, g128)
        ki2 = ikw2[:, :IDX_DIM]
        wi2 = ikw2[:, IDX_DIM:IDX_DIM + IDX_HEADS]
        n_phys = cache_diff_k.shape[1]
        da_s, sc_past, sc_new = _sample_diff_idx(
            page_table,
            jnp.transpose(cache_diff_k[l], (0, 2, 3, 4, 1)).reshape(n_phys, DQ_W, page),
            cache_diff_v[l].reshape(n_phys, page * DIFF_HEADS, DIFF_V),
            jnp.transpose(cache_idx_k[l], (0, 2, 1)),
            qd2.reshape(db, 1, DQ_W), kd2.reshape(db, 1, DQ_W),
            jnp.repeat(vd2.reshape(db, DIFF_HEADS, DIFF_V), 2, axis=1),
            qi2.reshape(db, IDX_HEADS, IDX_DIM), qi2.astype(F32).reshape(db, IDX_HEADS, IDX_DIM),
            ki2.reshape(db, 1, IDX_DIM),
            jnp.broadcast_to(wi2[:, :, None], (db, IDX_HEADS, LANES)),
            dec_d_last, dec_d_new, lam_p, subln, lam_init, g_pages)
        scores = jnp.concatenate(
            [sc_past.reshape(db, past), sc_new.reshape(db, LANES)[:, :1],
             jnp.full((db, LANES - 1), -jnp.inf, F32)], axis=1)
        mask = _sample_select(scores, k_sel_s)
        sa_s = _sample_dsa(
            page_table, cache_dsa_k[l], cache_dsa_v[l],
            qs2.reshape(db, DSA_HEADS, DSA_HD), ks2.reshape(db, 1, DSA_HD), vs2.reshape(db, 1, DSA_HD),
            mask.reshape(db, 1, past + LANES), dec_s_last, dec_s_new, g_pages)
        xs = _layer_tail(xs, da_s.reshape(db, A_WIDTH), sa_s.reshape(db, B_WIDTH), gate2,
                         p_sample[l, :, 0], tail_w)
        outs_s["dk"].append(kd2.reshape(db, dec_seq, DIFF_HEADS, 2, DIFF_QK))
        outs_s["dv"].append(vd2.reshape(db, dec_seq, DIFF_HEADS, DIFF_V))
        outs_s["sk"].append(ks2.reshape(db, dec_seq, DSA_HD))
        outs_s["sv"].append(vs2.reshape(db, dec_seq, DSA_HD))
        outs_s["ik"].append(ki2.reshape(db, dec_seq, IDX_DIM))

    st = jnp.stack
    return (xp.reshape(batch, seq, d_model), xs.reshape(db, dec_seq, d_model),
            st(outs_p["dk"]), st(outs_p["dv"]), st(outs_p["sk"]), st(outs_p["sv"]), st(outs_p["ik"]),
            st(outs_s["dk"]), st(outs_s["dv"]), st(outs_s["sk"]), st(outs_s["sv"]), st(outs_s["ik"]))
```

```python
import functools
import math

import numpy as np
import jax
import jax.numpy as jnp
from jax import lax
from jax.experimental import pallas as pl
from jax.experimental.pallas import tpu as pltpu

F32 = jnp.float32
BF16 = jnp.bfloat16
I32 = jnp.int32

DIFF_HEADS = 4
DIFF_QK = 64
DIFF_V = 128
DSA_HEADS = 4
DSA_HD = 128
IDX_HEADS = 4
IDX_DIM = 64
TOPK_MAX = 256
N_BUCKETS = 32
MAX_EXACT = 16
MAX_DIST = 128
PEER_HEADS = 8
PEER_NKEYS = 128
PEER_KH = 128
PEER_TOPK = 16
RMS_EPS = 1e-6
NEG_INF = -1e30

LANES = 128
VMEM_LIMIT = 56 * 1024 * 1024

DIFF_BLK = 512
DSA_ROWS = 256
DSA_CHUNK = 1024
DEC_PAGES = 32
PEER_TB = 512
PEER_EC = 1024

A_WIDTH = DIFF_HEADS * DIFF_V
B_WIDTH = DSA_HEADS * DSA_HD
DQ_W = DIFF_HEADS * 2 * DIFF_QK
IQ_W = IDX_HEADS * IDX_DIM

OFF_DQ = 0
OFF_DK = OFF_DQ + DQ_W
OFF_DV = OFF_DK + DQ_W
OFF_SQ = OFF_DV + A_WIDTH
OFF_SK = OFF_SQ + B_WIDTH
OFF_SV = OFF_SK + DSA_HD
OFF_IQ = OFF_SV + DSA_HD
OFF_IKW = OFF_IQ + IQ_W
OFF_G = OFF_IKW + LANES
IKW_USED = IDX_DIM + IDX_HEADS


def _bucket_table():
    d = np.arange(MAX_DIST)
    nf = np.maximum(d, 1).astype(np.float64)
    val = np.log(nf / MAX_EXACT) / math.log(MAX_DIST / MAX_EXACT) * (N_BUCKETS - MAX_EXACT)
    frac = np.abs(val - np.round(val))
    assert np.all(frac[MAX_EXACT + 1:] > 1e-4)
    large = MAX_EXACT + np.floor(val + 1e-9).astype(np.int64)
    return np.where(d < MAX_EXACT, d, np.minimum(large, N_BUCKETS - 1)).astype(np.int32)


_BUCKET = _bucket_table()


def _bucket_of(dist):
    dist = np.asarray(dist)
    return np.where(dist < MAX_DIST, _BUCKET[np.clip(dist, 0, MAX_DIST - 1)], N_BUCKETS - 1)


def _cparams(sem):
    return pltpu.CompilerParams(dimension_semantics=sem, vmem_limit_bytes=VMEM_LIMIT)


def _rms(x, gain):
    return x * lax.rsqrt(jnp.mean(x * x, axis=-1, keepdims=True) + RMS_EPS) * gain


def _proj_kernel(x_ref, g1_ref, w_ref, dqn_ref, dkn_ref, sqn_ref, skn_ref, g64_ref, g128_ref,
                 qd_ref, kd_ref, kdb_ref, vd_ref, vdb_ref, qs_ref, ks_ref, ksb_ref,
                 vs_ref, vsb_ref, qi_ref, ikw_ref, kib_ref, gate_ref):
    hb = _rms(x_ref[...], g1_ref[...]).astype(BF16)

    def seg(off, n):
        return jnp.dot(hb, w_ref[:, off:off + n], preferred_element_type=F32)

    def group_norm(z, gmat, gain):
        sq = z * z
        hi = sq.astype(BF16)
        lo = (sq - hi.astype(F32)).astype(BF16)
        ms = (jnp.dot(hi, gmat, preferred_element_type=F32)
              + jnp.dot(lo, gmat, preferred_element_type=F32))
        return z * lax.rsqrt(ms + RMS_EPS) * gain

    g64 = g64_ref[...]
    g128 = g128_ref[...]
    qd = group_norm(seg(OFF_DQ, DQ_W), g64, dqn_ref[...])
    qd_ref[...] = (qd * (DIFF_QK ** -0.5)).astype(BF16)
    kd = group_norm(seg(OFF_DK, DQ_W), g64, dkn_ref[...])
    kd_ref[...] = kd
    kdb_ref[...] = kd.astype(BF16)
    vd = seg(OFF_DV, A_WIDTH)
    vd_ref[...] = vd
    vdb_ref[...] = vd.astype(BF16)
    qs = group_norm(seg(OFF_SQ, B_WIDTH), g128, sqn_ref[...])
    qs_ref[...] = (qs * (DSA_HD ** -0.5)).astype(BF16)
    ks = _rms(seg(OFF_SK, DSA_HD), skn_ref[...])
    ks_ref[...] = ks
    ksb_ref[...] = ks.astype(BF16)
    vs = seg(OFF_SV, DSA_HD)
    vs_ref[...] = vs
    vsb_ref[...] = vs.astype(BF16)
    qi_ref[...] = seg(OFF_IQ, IQ_W).astype(BF16)
    ikw = seg(OFF_IKW, LANES)
    ikw_ref[...] = ikw
    kib_ref[...] = ikw.astype(BF16)
    d_model = x_ref.shape[1]
    for half in range(2):
        zg = seg(OFF_G + half * d_model, d_model)
        gate_ref[:, half * d_model:(half + 1) * d_model] = jax.nn.sigmoid(zg).astype(BF16)


def _proj(x, g1, w_pad, dqn, dkn, sqn, skn, g64, g128):
    t, d = x.shape
    tb = min(256, t)
    n_in = w_pad.shape[1]
    row = lambda n: pl.BlockSpec((tb, n), lambda i: (i, 0))
    full = lambda a: pl.BlockSpec(a.shape, lambda i: (0,) * a.ndim)
    outs = [
        (DQ_W, BF16),
        (DQ_W, F32), (DQ_W, BF16),
        (A_WIDTH, F32), (A_WIDTH, BF16),
        (B_WIDTH, BF16),
        (DSA_HD, F32), (DSA_HD, BF16),
        (DSA_HD, F32), (DSA_HD, BF16),
        (IQ_W, BF16),
        (LANES, F32), (LANES, BF16),
        (2 * d, BF16),
    ]
    return pl.pallas_call(
        _proj_kernel,
        grid=(t // tb,),
        in_specs=[row(d), full(g1), full(w_pad), full(dqn), full(dkn), full(sqn), full(skn),
                  full(g64), full(g128)],
        out_specs=[row(n) for n, _ in outs],
        out_shape=[jax.ShapeDtypeStruct((t, n), dt) for n, dt in outs],
        compiler_params=_cparams(("parallel",)),
        name="proj",
    )(x, g1, w_pad, dqn, dkn, sqn, skn, g64, g128)


def _diff_lambda(lp, lam_init):
    e1 = jnp.exp(jnp.sum(lp[0:1] * lp[1:2], axis=-1, keepdims=True))
    e2 = jnp.exp(jnp.sum(lp[2:3] * lp[3:4], axis=-1, keepdims=True))
    return e1 - e2 + lam_init


def _diff_attn_kernel(q_ref, kt_ref, v_ref, bd_ref, bp_ref, lamp_ref, sub_ref, o_ref,
                      m_sc, l_sc, acc_sc, *, lam_init):
    qi = pl.program_id(1)
    blk = q_ref.shape[2]
    n_rep = blk // LANES
    m_sc[...] = jnp.full(m_sc.shape, NEG_INF, F32)
    l_sc[...] = jnp.zeros(l_sc.shape, F32)
    acc_sc[...] = jnp.zeros(acc_sc.shape, F32)

    def chunk(j, kind):
        off = pl.multiple_of(j * blk, blk)
        v = v_ref[0, pl.ds(off, blk), :]
        for c in range(2):
            s = jnp.dot(q_ref[0, c], kt_ref[0, c, :, pl.ds(off, blk)], preferred_element_type=F32)
            if kind == 1:
                s = s + bp_ref[0]
            if kind == 2:
                r = lax.broadcasted_iota(I32, (blk, blk), 0)
                cc = lax.broadcasted_iota(I32, (blk, blk), 1)
                s = jnp.where(r >= cc, s + bd_ref[0], NEG_INF)
            m_prev = m_sc[c]
            m_new = jnp.maximum(m_prev, jnp.max(s, axis=-1, keepdims=True))
            alpha = jnp.exp(m_prev - m_new)
            p = jnp.exp(s - jnp.tile(m_new, (1, n_rep)))
            l_sc[c] = alpha * l_sc[c] + jnp.sum(p, axis=-1, keepdims=True)
            acc_sc[c] = alpha * acc_sc[c] + jnp.dot(p.astype(BF16), v, preferred_element_type=F32)
            m_sc[c] = m_new

    def far(j, carry):
        chunk(j, 0)
        return carry

    lax.fori_loop(0, jnp.maximum(qi - 1, 0), far, 0)
    pl.when(qi >= 1)(lambda: chunk(qi - 1, 1))
    chunk(qi, 2)
    lam = _diff_lambda(lamp_ref[...], lam_init)
    o = acc_sc[0] / l_sc[0] - lam * (acc_sc[1] / l_sc[1])
    o_ref[...] = (_rms(o, sub_ref[...]) * (1.0 - lam_init)).astype(o_ref.dtype)


def _diff_attn(q4, kt4, v3, bias_d, bias_p, lam_p, subln, lam_init, blk):
    h, _, t, dqk = q4.shape
    dv = v3.shape[2]
    assert dv == LANES
    kern = functools.partial(_diff_attn_kernel, lam_init=lam_init)
    return pl.pallas_call(
        kern,
        grid=(h, t // blk),
        in_specs=[
            pl.BlockSpec((1, 2, blk, dqk), lambda hh, qi: (hh, 0, qi, 0)),
            pl.BlockSpec((1, 2, dqk, t), lambda hh, qi: (hh, 0, 0, 0)),
            pl.BlockSpec((1, t, dv), lambda hh, qi: (hh, 0, 0)),
            pl.BlockSpec((1, blk, blk), lambda hh, qi: (hh, 0, 0)),
            pl.BlockSpec((1, blk, blk), lambda hh, qi: (hh, 0, 0)),
            pl.BlockSpec(lam_p.shape, lambda hh, qi: (0, 0)),
            pl.BlockSpec(subln.shape, lambda hh, qi: (0, 0)),
        ],
        out_specs=pl.BlockSpec((blk, dv), lambda hh, qi: (qi, hh)),
        out_shape=jax.ShapeDtypeStruct((t, h * dv), BF16),
        scratch_shapes=[pltpu.VMEM((2, blk, LANES), F32), pltpu.VMEM((2, blk, LANES), F32),
                        pltpu.VMEM((2, blk, dv), F32)],
        compiler_params=_cparams(("parallel", "arbitrary")),
        name="diff_attn",
    )(q4, kt4, v3, bias_d, bias_p, lam_p, subln)


def _sort_key(s):
    s = jnp.where(s == 0.0, 0.0, s)
    bits = pltpu.bitcast(s, I32)
    return jnp.where(bits < 0, bits ^ jnp.int32(0x7FFFFFFF), bits)


ROW_GROUP = 64


def _count_keys(keys_sc, r0, rg, n_ch, ch, pred):
    def body(c, acc):
        off = pl.multiple_of(c * ch, ch)
        for j in range(ch // LANES):
            kt = keys_sc[r0:r0 + rg, pl.ds(off + j * LANES, LANES)]
            acc = acc + jnp.where(pred(kt), 1.0, 0.0)
        return acc

    acc = lax.fori_loop(0, n_ch, body, jnp.zeros((rg, LANES), F32))
    return jnp.sum(acc, axis=-1, keepdims=True)


def _topk_threshold(keys_sc, rows, n_ch, ch, k_sel):
    ts, ms = [], []
    for r0 in range(0, rows, ROW_GROUP):
        rg = min(ROW_GROUP, rows - r0)

        def bis(state, r0=r0, rg=rg):
            i, t, done = state
            cand = t ^ jnp.left_shift(jnp.int32(1), 31 - i)
            cand_b = jnp.broadcast_to(cand, (rg, LANES))
            cnt = _count_keys(keys_sc, r0, rg, n_ch, ch, lambda kt: kt >= cand_b)
            t = jnp.where(done > 0.5, t, jnp.where(cnt >= float(k_sel), cand, t))
            return i + 1, t, jnp.where(cnt == float(k_sel), 1.0, done)

        def unfinished(state):
            i, _, done = state
            return jnp.logical_and(i < 32, jnp.min(done) < 0.5)

        _, t, _ = lax.while_loop(unfinished, bis, (jnp.int32(0), jnp.full((rg, 1), -2 ** 31, I32),
                                                   jnp.zeros((rg, 1), F32)))
        t_b = jnp.broadcast_to(t, (rg, LANES))
        c_gt = _count_keys(keys_sc, r0, rg, n_ch, ch, lambda kt: kt > t_b)
        ts.append(t_b)
        ms.append(jnp.broadcast_to(float(k_sel) - c_gt, (rg, LANES)))
    return jnp.concatenate(ts, axis=0), jnp.concatenate(ms, axis=0)


def _topk_threshold_short(keys_sc, rows, k_sel):
    n_tiles = keys_sc.shape[1] // LANES
    groups = [(r0, min(ROW_GROUP, rows - r0)) for r0 in range(0, rows, ROW_GROUP)]

    def count(r0, rg, pred):
        acc = jnp.zeros((rg, LANES), F32)
        for j in range(n_tiles):
            acc = acc + jnp.where(pred(keys_sc[r0:r0 + rg, j * LANES:(j + 1) * LANES]), 1.0, 0.0)
        return jnp.sum(acc, axis=-1, keepdims=True)

    def bis(i, ts):
        bit = jnp.left_shift(jnp.int32(1), 31 - i)
        out = []
        for (r0, rg), t in zip(groups, ts):
            cand_b = jnp.broadcast_to(t ^ bit, (rg, LANES))
            cnt = count(r0, rg, lambda kt: kt >= cand_b)
            out.append(jnp.where(cnt >= float(k_sel), t ^ bit, t))
        return tuple(out)

    ts = lax.fori_loop(0, 32, bis, tuple(jnp.full((rg, 1), -2 ** 31, I32) for _, rg in groups))
    t_bs, m_bs = [], []
    for (r0, rg), t in zip(groups, ts):
        t_b = jnp.broadcast_to(t, (rg, LANES))
        c_gt = count(r0, rg, lambda kt: kt > t_b)
        t_bs.append(t_b)
        m_bs.append(jnp.broadcast_to(float(k_sel) - c_gt, (rg, LANES)))
    return jnp.concatenate(t_bs, axis=0), jnp.concatenate(m_bs, axis=0)


CAND_DEPTH = 12
SUBLANES = 8


def _lane_top_keys(keys_sc, cand_sc, rows, n_ch, ch):
    lowest = jnp.full((SUBLANES, LANES), -jnp.inf, F32)

    def flip(k):
        return jnp.where(k < 0, k ^ jnp.int32(0x7FFFFFFF), k)

    def group(g, _):
        r0 = pl.multiple_of(g * SUBLANES, SUBLANES)

        def chunk(c, regs):
            off = pl.multiple_of(c * ch, ch)
            for j in range(ch // LANES):
                x = pltpu.bitcast(flip(keys_sc[pl.ds(r0, SUBLANES), pl.ds(off + j * LANES, LANES)]), F32)
                new = []
                for reg in regs:
                    new.append(jnp.maximum(reg, x))
                    x = jnp.minimum(reg, x)
                regs = tuple(new)
            return regs

        regs = lax.fori_loop(0, n_ch, chunk, (lowest,) * CAND_DEPTH)
        for i, reg in enumerate(regs):
            cand_sc[pl.ds(r0, SUBLANES), i * LANES:(i + 1) * LANES] = flip(pltpu.bitcast(reg, I32))
        return 0

    lax.fori_loop(0, rows // SUBLANES, group, 0)


def _tie_constants():
    r = lax.broadcasted_iota(I32, (LANES, LANES), 0)
    c = lax.broadcasted_iota(I32, (LANES, LANES), 1)
    tri = jnp.where(r < c, 1.0, 0.0).astype(BF16)
    ones = jnp.ones((LANES, LANES), BF16)
    return tri, ones


def _select_tile(kt, t_b, m_b, carry, tri, ones):
    eq = kt == t_b
    eqb = jnp.where(eq, 1.0, 0.0).astype(BF16)
    rank = carry + jnp.dot(eqb, tri, preferred_element_type=F32)
    sel = jnp.where(kt > t_b, 1.0, jnp.where(eq, jnp.where(rank < m_b, 1.0, 0.0), 0.0))
    return sel, carry + jnp.dot(eqb, ones, preferred_element_type=F32)


def _dsa_prompt_kernel(qi_ref, ikw_ref, qs_ref, kit_ref, kst_ref, vs_ref, b3_ref,
                       o_ref, keys_sc, cand_sc, msk_sc, m_sc, l_sc, acc_sc, *, k_sel, ch):
    b = pl.program_id(0)
    rows = o_ref.shape[0]
    sub = ch // LANES
    rt = rows // LANES
    n_ch = ((b + 1) * rows + ch - 1) // ch
    row_pos = b * rows + lax.broadcasted_iota(I32, (rows, 1), 0)

    w = ikw_ref[:, IDX_DIM:IDX_DIM + IDX_HEADS]

    def score_chunk(c, _):
        off = pl.multiple_of(c * ch, ch)
        kt = kit_ref[:, pl.ds(off, ch)]
        s = jnp.zeros((rows, ch), F32)
        for h in range(IDX_HEADS):
            d = jnp.dot(qi_ref[h], kt, preferred_element_type=F32)
            s = s + w[:, h:h + 1] * jnp.maximum(d, 0.0)
        col_pos = off + lax.broadcasted_iota(I32, (1, ch), 1)
        s = jnp.where(col_pos <= row_pos, s, NEG_INF)
        keys_sc[:, pl.ds(off, ch)] = _sort_key(s)
        return 0

    lax.fori_loop(0, n_ch, score_chunk, 0)

    _lane_top_keys(keys_sc, cand_sc, rows, n_ch, ch)
    t_b, m_b = _topk_threshold_short(cand_sc, rows, k_sel)
    dropped_above = jnp.max(jnp.where(cand_sc[:, (CAND_DEPTH - 1) * LANES:] > t_b, 1.0, 0.0))
    t_b, m_b = lax.cond(dropped_above < 0.5, lambda: (t_b, m_b),
                        lambda: _topk_threshold(keys_sc, rows, n_ch, ch, k_sel))

    m_sc[...] = jnp.full(m_sc.shape, NEG_INF, F32)
    l_sc[...] = jnp.zeros(l_sc.shape, F32)
    acc_sc[...] = jnp.zeros(acc_sc.shape, F32)
    tri, ones = _tie_constants()
    first_near = rt * b - 1

    def attend_chunk(c, carry, near):
        off = pl.multiple_of(c * ch, ch)
        for j in range(sub):
            kt = keys_sc[:, pl.ds(off + j * LANES, LANES)]
            sel, carry = _select_tile(kt, t_b, m_b, carry, tri, ones)
            if near:
                col_pos = off + j * LANES + lax.broadcasted_iota(I32, (1, LANES), 1)
                sel = jnp.where(col_pos <= row_pos, sel, 0.0)
            msk_sc[:, j * LANES:(j + 1) * LANES] = jnp.where(sel > 0.5, 0.0, NEG_INF)
        kst = kst_ref[:, pl.ds(off, ch)]
        vch = vs_ref[pl.ds(off, ch), :]
        for h in range(DSA_HEADS):
            s = jnp.dot(qs_ref[:, h * DSA_HD:(h + 1) * DSA_HD], kst, preferred_element_type=F32)
            if near:
                parts = []
                for j in range(sub):
                    rel = c * sub + j - first_near
                    bias = jnp.zeros((rows, LANES), F32)
                    for k in range(rt + 1):
                        bias = jnp.where(rel == k, b3_ref[h, k], bias)
                    parts.append(s[:, j * LANES:(j + 1) * LANES] + bias)
                s = jnp.concatenate(parts, axis=1)
            s = s + msk_sc[...]
            m_prev = m_sc[h]
            m_new = jnp.maximum(m_prev, jnp.max(s, axis=-1, keepdims=True))
            alpha = jnp.exp(m_prev - m_new)
            p = jnp.exp(s - jnp.tile(m_new, (1, sub)))
            l_sc[h] = alpha * l_sc[h] + jnp.sum(p, axis=-1, keepdims=True)
            acc_sc[h] = alpha * acc_sc[h] + jnp.dot(p.astype(BF16), vch, preferred_element_type=F32)
            m_sc[h] = m_new
        return carry

    c_near = (jnp.maximum(first_near, 0) * LANES) // ch
    carry = lax.fori_loop(0, c_near, lambda c, cr: attend_chunk(c, cr, False),
                          jnp.zeros((rows, LANES), F32))
    lax.fori_loop(c_near, n_ch, lambda c, cr: attend_chunk(c, cr, True), carry)
    for h in range(DSA_HEADS):
        o_ref[:, h * DSA_HD:(h + 1) * DSA_HD] = (acc_sc[h] / l_sc[h]).astype(o_ref.dtype)


def _dsa_prompt(qi4, ikw, qs, kit, kst, vsb, bias3, k_sel, rows, ch):
    t = qs.shape[0]
    assert DSA_HD == LANES and t % ch == 0 and ch % rows == 0
    full = lambda a: pl.BlockSpec(a.shape, lambda i: (0,) * a.ndim)
    kern = functools.partial(_dsa_prompt_kernel, k_sel=k_sel, ch=ch)
    return pl.pallas_call(
        kern,
        grid=(t // rows,),
        in_specs=[
            pl.BlockSpec((IDX_HEADS, rows, IDX_DIM), lambda i: (0, i, 0)),
            pl.BlockSpec((rows, LANES), lambda i: (i, 0)),
            pl.BlockSpec((rows, B_WIDTH), lambda i: (i, 0)),
            full(kit), full(kst), full(vsb), full(bias3),
        ],
        out_specs=pl.BlockSpec((rows, B_WIDTH), lambda i: (i, 0)),
        out_shape=jax.ShapeDtypeStruct((t, B_WIDTH), BF16),
        scratch_shapes=[pltpu.VMEM((rows, t), I32), pltpu.VMEM((rows, CAND_DEPTH * LANES), I32),
                        pltpu.VMEM((rows, ch), F32),
                        pltpu.VMEM((DSA_HEADS, rows, LANES), F32),
                        pltpu.VMEM((DSA_HEADS, rows, LANES), F32),
                        pltpu.VMEM((DSA_HEADS, rows, DSA_HD), F32)],
        compiler_params=_cparams(("parallel",)),
        name="dsa_prompt",
    )(qi4, ikw, qs, kit, kst, vsb, bias3)


def _sample_diff_idx_kernel(pt_ref, *refs, lam_init, g_pages):
    n = g_pages
    k_refs = refs[0:n]
    v_refs = refs[n:2 * n]
    ik_refs = refs[2 * n:3 * n]
    (qd_ref, kd_ref, vd_ref, qib_ref, qif_ref, ki_ref, wi_ref, bl_ref, bn_ref, lamp_ref,
     sub_ref) = refs[3 * n:3 * n + 11]
    o_ref, sc_ref, sn_ref = refs[3 * n + 11:3 * n + 14]
    m_sc, l_sc, acc_sc = refs[3 * n + 14:]
    g = pl.program_id(1)
    ng = pl.num_programs(1)
    nrow = 2 * DIFF_HEADS

    @pl.when(g == 0)
    def _():
        m_sc[...] = jnp.full(m_sc.shape, NEG_INF, F32)
        l_sc[...] = jnp.zeros(l_sc.shape, F32)
        acc_sc[...] = jnp.zeros(acc_sc.shape, F32)

    lane_grp = lax.broadcasted_iota(I32, (nrow, DQ_W), 1) // DIFF_QK
    row_id = lax.broadcasted_iota(I32, (nrow, DQ_W), 0)
    qf = jnp.where(lane_grp == row_id, jnp.broadcast_to(qd_ref[0].astype(F32), (nrow, DQ_W)), 0.0)
    qbd = qf.astype(BF16)

    logits = []
    for j in range(n):
        s = jnp.dot(qbd, k_refs[j][0].astype(BF16), preferred_element_type=F32)
        if j == n - 1:
            s = s + jnp.where(g == ng - 1, bl_ref[...], 0.0)
        logits.append(s)
        d = jnp.dot(qib_ref[0], ik_refs[j][0].astype(BF16), preferred_element_type=F32)
        srow = jnp.sum(wi_ref[0] * jnp.maximum(d, 0.0), axis=0, keepdims=True)
        off = pl.multiple_of((g * n + j) * LANES, LANES)
        sc_ref[0, :, pl.ds(off, LANES)] = srow
    s = jnp.concatenate(logits, axis=1)
    m_prev = m_sc[...]
    m_new = jnp.maximum(m_prev, jnp.max(s, axis=-1, keepdims=True))
    alpha = jnp.exp(m_prev - m_new)
    p = jnp.exp(s - m_new)
    l_sc[...] = alpha * l_sc[...] + jnp.sum(p, axis=-1, keepdims=True)
    pv = []
    for h in range(DIFF_HEADS):
        acc_h = jnp.zeros((2, DIFF_V), F32)
        for j in range(n):
            vh = v_refs[j][0, pl.ds(h, LANES, stride=DIFF_HEADS), :].astype(BF16)
            ph = p[2 * h:2 * h + 2, j * LANES:(j + 1) * LANES].astype(BF16)
            acc_h = acc_h + jnp.dot(ph, vh, preferred_element_type=F32)
        pv.append(acc_h)
    acc_sc[...] = alpha * acc_sc[...] + jnp.concatenate(pv, axis=0)
    m_sc[...] = m_new

    @pl.when(g == ng - 1)
    def _():
        s_new = jnp.sum(qf * kd_ref[0], axis=-1, keepdims=True) + bn_ref[:, 0:1]
        m_prev = m_sc[...]
        m_new = jnp.maximum(m_prev, s_new)
        alpha = jnp.exp(m_prev - m_new)
        p_new = jnp.exp(s_new - m_new)
        l_fin = alpha * l_sc[...] + p_new
        acc = (alpha * acc_sc[...] + p_new * vd_ref[0]) / l_fin
        lam = _diff_lambda(lamp_ref[...], lam_init)
        for h in range(DIFF_HEADS):
            o = acc[2 * h:2 * h + 1] - lam * acc[2 * h + 1:2 * h + 2]
            o_ref[0, :, h * DIFF_V:(h + 1) * DIFF_V] = (
                _rms(o, sub_ref[...]) * (1.0 - lam_init)).astype(o_ref.dtype)
        d_new = jnp.sum(qif_ref[0] * ki_ref[0], axis=-1, keepdims=True)
        s_idx = jnp.sum(wi_ref[0][:, 0:1] * jnp.maximum(d_new, 0.0), axis=0, keepdims=True)
        sn_ref[0] = jnp.broadcast_to(s_idx, (1, LANES))


def _sample_diff_idx(page_table, ck, cv, cik, qd, kd, vd, qib, qif, ki, wi_b, bias_last, bias_new,
                     lam_p, subln, lam_init, g_pages):
    db, n_pages = page_table.shape
    page = ck.shape[2]
    past = n_pages * page
    n = g_pages
    ng = n_pages // n

    def page_spec(a, j):
        return pl.BlockSpec((1,) + a.shape[1:], lambda s, g, pt: (pt[s, g * n + j], 0, 0))

    per_s = lambda a: pl.BlockSpec((1,) + a.shape[1:], lambda s, g, pt: (s,) + (0,) * (a.ndim - 1))
    full = lambda a: pl.BlockSpec(a.shape, lambda s, g, pt: (0,) * a.ndim)
    in_specs = ([page_spec(ck, j) for j in range(n)]
                + [page_spec(cv, j) for j in range(n)]
                + [page_spec(cik, j) for j in range(n)]
                + [per_s(qd), per_s(kd), per_s(vd), per_s(qib), per_s(qif), per_s(ki), per_s(wi_b),
                   full(bias_last), full(bias_new), full(lam_p), full(subln)])
    kern = functools.partial(_sample_diff_idx_kernel, lam_init=lam_init, g_pages=n)
    nrow = 2 * DIFF_HEADS
    grid_spec = pltpu.PrefetchScalarGridSpec(
        num_scalar_prefetch=1,
        grid=(db, ng),
        in_specs=in_specs,
        out_specs=[pl.BlockSpec((1, 1, A_WIDTH), lambda s, g, pt: (s, 0, 0)),
                   pl.BlockSpec((1, 1, past), lambda s, g, pt: (s, 0, 0)),
                   pl.BlockSpec((1, 1, LANES), lambda s, g, pt: (s, 0, 0))],
        scratch_shapes=[pltpu.VMEM((nrow, 1), F32), pltpu.VMEM((nrow, 1), F32),
                        pltpu.VMEM((nrow, DIFF_V), F32)],
    )
    return pl.pallas_call(
        kern,
        grid_spec=grid_spec,
        out_shape=[jax.ShapeDtypeStruct((db, 1, A_WIDTH), BF16),
                   jax.ShapeDtypeStruct((db, 1, past), F32),
                   jax.ShapeDtypeStruct((db, 1, LANES), F32)],
        compiler_params=_cparams(("parallel", "arbitrary")),
        name="sample_diff_idx",
    )(page_table, *([ck] * n), *([cv] * n), *([cik] * n), qd, kd, vd, qib, qif, ki, wi_b,
      bias_last, bias_new, lam_p, subln)


def _sample_select_kernel(s_ref, o_ref, keys_sc, *, k_sel):
    rows, nk = s_ref.shape
    n_tiles = nk // LANES

    def to_keys(j, _):
        off = pl.multiple_of(j * LANES, LANES)
        keys_sc[:, pl.ds(off, LANES)] = _sort_key(s_ref[:, pl.ds(off, LANES)])
        return 0

    lax.fori_loop(0, n_tiles, to_keys, 0)
    sub = max(d for d in range(1, 9) if n_tiles % d == 0)
    t_b, m_b = _topk_threshold(keys_sc, rows, n_tiles // sub, sub * LANES, k_sel)
    tri, ones = _tie_constants()

    def emit(j, carry):
        off = pl.multiple_of(j * LANES, LANES)
        sel, carry = _select_tile(keys_sc[:, pl.ds(off, LANES)], t_b, m_b, carry, tri, ones)
        o_ref[:, pl.ds(off, LANES)] = jnp.where(sel > 0.5, 0.0, NEG_INF)
        return carry

    lax.fori_loop(0, n_tiles, emit, jnp.zeros((rows, LANES), F32))


def _sample_select(scores, k_sel):
    rows, nk = scores.shape
    return pl.pallas_call(
        functools.partial(_sample_select_kernel, k_sel=k_sel),
        out_shape=jax.ShapeDtypeStruct((rows, nk), F32),
        scratch_shapes=[pltpu.VMEM((rows, nk), I32)],
        compiler_params=pltpu.CompilerParams(vmem_limit_bytes=VMEM_LIMIT),
        name="sample_select",
    )(scores)


def _sample_dsa_kernel(pt_ref, *refs, g_pages):
    n = g_pages
    k_refs = refs[0:n]
    v_refs = refs[n:2 * n]
    qs_ref, ks_ref, vs_ref, msk_ref, bl_ref, bn_ref = refs[2 * n:2 * n + 6]
    o_ref = refs[2 * n + 6]
    m_sc, l_sc, acc_sc = refs[2 * n + 7:]
    g = pl.program_id(1)
    ng = pl.num_programs(1)

    @pl.when(g == 0)
    def _():
        m_sc[...] = jnp.full(m_sc.shape, NEG_INF, F32)
        l_sc[...] = jnp.zeros(l_sc.shape, F32)
        acc_sc[...] = jnp.zeros(acc_sc.shape, F32)

    q = qs_ref[0]
    logits = []
    for j in range(n):
        s = lax.dot_general(q, k_refs[j][0].astype(BF16), (((1,), (1,)), ((), ())),
                            preferred_element_type=F32)
        if j == n - 1:
            s = s + jnp.where(g == ng - 1, bl_ref[...], 0.0)
        off = pl.multiple_of((g * n + j) * LANES, LANES)
        logits.append(s + msk_ref[0, :, pl.ds(off, LANES)])
    s = jnp.concatenate(logits, axis=1)
    m_prev = m_sc[...]
    m_new = jnp.maximum(m_prev, jnp.max(s, axis=-1, keepdims=True))
    alpha = jnp.exp(m_prev - m_new)
    p = jnp.exp(s - m_new)
    l_sc[...] = alpha * l_sc[...] + jnp.sum(p, axis=-1, keepdims=True)
    pv = jnp.zeros(acc_sc.shape, F32)
    for j in range(n):
        pv = pv + jnp.dot(p[:, j * LANES:(j + 1) * LANES].astype(BF16), v_refs[j][0].astype(BF16),
                          preferred_element_type=F32)
    acc_sc[...] = alpha * acc_sc[...] + pv
    m_sc[...] = m_new

    @pl.when(g == ng - 1)
    def _():
        past = ng * n * LANES
        s_new = (jnp.sum(q.astype(F32) * ks_ref[0], axis=-1, keepdims=True) + bn_ref[:, 0:1]
                 + msk_ref[0, :, past:past + 1])
        m_prev = m_sc[...]
        m_new = jnp.maximum(m_prev, s_new)
        alpha = jnp.exp(m_prev - m_new)
        p_new = jnp.exp(s_new - m_new)
        l_fin = alpha * l_sc[...] + p_new
        acc = alpha * acc_sc[...] + p_new * vs_ref[0]
        o_ref[0] = (acc / l_fin).astype(o_ref.dtype)


def _sample_dsa(page_table, ck, cv, qs, ks, vs, mask, bias_last, bias_new, g_pages):
    db, n_pages = page_table.shape
    page = ck.shape[1]
    n = g_pages
    ng = n_pages // n

    def page_spec(width, j):
        return pl.BlockSpec((1, page, width), lambda s, g, pt: (pt[s, g * n + j], 0, 0))

    per_s = lambda a: pl.BlockSpec((1,) + a.shape[1:], lambda s, g, pt: (s,) + (0,) * (a.ndim - 1))
    full = lambda a: pl.BlockSpec(a.shape, lambda s, g, pt: (0,) * a.ndim)
    in_specs = ([page_spec(ck.shape[2], j) for j in range(n)]
                + [page_spec(cv.shape[2], j) for j in range(n)]
                + [per_s(qs), per_s(ks), per_s(vs), per_s(mask), full(bias_last), full(bias_new)])
    grid_spec = pltpu.PrefetchScalarGridSpec(
        num_scalar_prefetch=1,
        grid=(db, ng),
        in_specs=in_specs,
        out_specs=pl.BlockSpec((1, DSA_HEADS, DSA_HD), lambda s, g, pt: (s, 0, 0)),
        scratch_shapes=[pltpu.VMEM((DSA_HEADS, 1), F32), pltpu.VMEM((DSA_HEADS, 1), F32),
                        pltpu.VMEM((DSA_HEADS, DSA_HD), F32)],
    )
    return pl.pallas_call(
        functools.partial(_sample_dsa_kernel, g_pages=n),
        grid_spec=grid_spec,
        out_shape=jax.ShapeDtypeStruct((db, DSA_HEADS, DSA_HD), BF16),
        compiler_params=_cparams(("parallel", "arbitrary")),
        name="sample_dsa",
    )(page_table, *([ck] * n), *([cv] * n), qs, ks, vs, mask, bias_last, bias_new)


def _tail_a_kernel(x_ref, da_ref, sa_ref, gate_ref, wa_ref, wb_ref, wo_ref, n2_ref, x1_ref, h2_ref):
    d = x_ref.shape[1]
    ya = jnp.dot(da_ref[...], wa_ref[...], preferred_element_type=F32)
    yb = jnp.dot(sa_ref[...], wb_ref[...], preferred_element_type=F32)
    merged = gate_ref[:, :d].astype(F32) * ya + gate_ref[:, d:].astype(F32) * yb
    x1 = x_ref[...] + jnp.dot(merged.astype(BF16), wo_ref[...], preferred_element_type=F32)
    x1_ref[...] = x1
    h2_ref[...] = _rms(x1, n2_ref[...]).astype(BF16)


def _tail_a(x, da, sa, gate, wa, wb, wo, n2):
    t, d = x.shape
    tb = min(256, t)
    row = lambda n: pl.BlockSpec((tb, n), lambda i: (i, 0))
    full = lambda a: pl.BlockSpec(a.shape, lambda i: (0,) * a.ndim)
    return pl.pallas_call(
        _tail_a_kernel,
        grid=(t // tb,),
        in_specs=[row(d), row(da.shape[1]), row(sa.shape[1]), row(2 * d), full(wa), full(wb),
                  full(wo), full(n2)],
        out_specs=[row(d), row(d)],
        out_shape=[jax.ShapeDtypeStruct((t, d), F32), jax.ShapeDtypeStruct((t, d), BF16)],
        compiler_params=_cparams(("parallel",)),
        name="tail_a",
    )(x, da, sa, gate, wa, wb, wo, n2)


def _top_rows(work, n_top, store, want_rank=False):
    rows = work.shape[0]
    iota = lax.broadcasted_iota(I32, work.shape, 0).astype(F32)
    rank = jnp.full(work.shape, float(rows), F32) if want_rank else None
    mx = None
    for r in range(n_top):
        mx = jnp.max(work, axis=0, keepdims=True)
        store(r, mx)
        first = jnp.min(jnp.where(work == mx, iota, float(rows)), axis=0, keepdims=True)
        hit = iota == first
        work = jnp.where(hit, -jnp.inf, work)
        if want_rank:
            rank = jnp.where(hit, float(r), rank)
    return mx, rank


def _staircase_pairs():
    return [(i, j) for i in range(PEER_TOPK) for j in range(PEER_TOPK // (i + 1))]


def _peer_route_kernel(h_ref, wq_ref, keys_ref, a_ref, n_ref, b_ref, r2_ref, q_sc, top_sc, cand_sc):
    tb = h_ref.shape[0]
    q_sc[...] = jnp.dot(h_ref[...], wq_ref[...], preferred_element_type=F32).astype(BF16)
    pairs = _staircase_pairs()
    n_cand = cand_sc.shape[0]
    cand_sc[len(pairs):, :] = jnp.full((n_cand - len(pairs), tb), -jnp.inf, F32)

    def head(h, _):
        expo, ranks = [], []
        for c in range(2):
            off = pl.multiple_of((h * 2 + c) * PEER_KH, PEER_KH)
            s = lax.dot_general(keys_ref[c, h], q_sc[:, pl.ds(off, PEER_KH)],
                                (((1,), (1,)), ((), ())), preferred_element_type=F32)

            def store(r, row, c=c):
                top_sc[c, r:r + 1, :] = row

            _, rank = _top_rows(s, PEER_TOPK, store, want_rank=True)
            ranks.append(rank)
            expo.append(jnp.exp(s - top_sc[c, 0:1, :]))
        for r, (i, j) in enumerate(pairs):
            cand_sc[r:r + 1, :] = top_sc[0, i:i + 1, :] + top_sc[1, j:j + 1, :]
        cs = cand_sc[...]
        theta, _ = _top_rows(cs, PEER_TOPK, lambda r, row: None)
        top_sum = top_sc[0, 0:1, :] + top_sc[1, 0:1, :]
        keep = cs >= theta
        z = jnp.sum(jnp.where(keep, jnp.exp(cs - top_sum), 0.0), axis=0, keepdims=True)
        n = jnp.zeros((PEER_NKEYS, tb), F32)
        r0 = 0
        for i in range(PEER_TOPK):
            width = PEER_TOPK // (i + 1)
            n_i = jnp.sum(jnp.where(cand_sc[r0:r0 + width, :] >= theta, 1.0, 0.0),
                          axis=0, keepdims=True)
            n = jnp.where(ranks[0] == float(i), n_i, n)
            r0 += width
        a_ref[h] = expo[0]
        n_ref[h] = n
        b_ref[h] = (expo[1] / z).astype(BF16)
        r2_ref[h] = ranks[1].astype(BF16)
        return 0

    lax.fori_loop(0, PEER_HEADS, head, 0)


def _peer_route(h2, wq, keys):
    t, d = h2.shape
    tb = min(256, t)
    full = lambda a: pl.BlockSpec(a.shape, lambda i: (0,) * a.ndim)
    big = pl.BlockSpec((PEER_HEADS, PEER_NKEYS, tb), lambda i: (0, 0, i))
    shape = lambda dt: jax.ShapeDtypeStruct((PEER_HEADS, PEER_NKEYS, t), dt)
    n_cand = -(-len(_staircase_pairs()) // 8) * 8
    return pl.pallas_call(
        _peer_route_kernel,
        grid=(t // tb,),
        in_specs=[pl.BlockSpec((tb, d), lambda i: (i, 0)), full(wq), full(keys)],
        out_specs=[big, big, big, big],
        out_shape=[shape(F32), shape(F32), shape(BF16), shape(BF16)],
        scratch_shapes=[pltpu.VMEM((tb, wq.shape[1]), BF16), pltpu.VMEM((2, PEER_TOPK, tb), F32),
                        pltpu.VMEM((n_cand, tb), F32)],
        compiler_params=_cparams(("parallel",)),
        name="peer_route",
    )(h2, wq, keys)


BF16_SUBLANES = 16


def _bcast_rows_bf16(row, rows):
    one = jnp.broadcast_to(row, (BF16_SUBLANES, row.shape[1])).astype(BF16)
    return jnp.tile(one, (rows // BF16_SUBLANES, 1))


def _gelu(x):
    return 0.5 * x * (1.0 + lax.erf(x * math.sqrt(0.5)))


def _peer_dense_kernel(h_ref, eu_ref, evt_ref, a_ref, n_ref, b_ref, r2_ref, o_ref, w_sc, p_sc):
    e = pl.program_id(1)
    last = pl.num_programs(1) - 2
    ec = eu_ref.shape[0]
    n_i1 = ec // PEER_NKEYS
    tb = h_ref.shape[0]
    tw = min(tb, 2 * LANES)
    slot = e % 2

    @pl.when(e == 0)
    def _():
        o_ref[...] = jnp.zeros(o_ref.shape, F32)
        p_sc[1] = jnp.zeros(p_sc.shape[1:], BF16)

    o_ref[...] += jnp.dot(evt_ref[...], p_sc[1 - slot], preferred_element_type=F32)

    cur = jnp.minimum(e, last)
    for jj in range(n_i1):
        i1 = cur * n_i1 + jj
        rows = slice(jj * PEER_NKEYS, (jj + 1) * PEER_NKEYS)
        for t0 in range(0, tb, tw):
            cols = slice(t0, t0 + tw)
            w = jnp.zeros((PEER_NKEYS, tw), BF16)
            for h in range(PEER_HEADS):
                a_b = _bcast_rows_bf16(a_ref[h, pl.ds(i1, 1), cols], PEER_NKEYS)
                n_b = _bcast_rows_bf16(n_ref[h, pl.ds(i1, 1), cols], PEER_NKEYS)
                w = w + jnp.where(r2_ref[h, :, cols] < n_b, a_b * b_ref[h, :, cols],
                                  jnp.zeros((), BF16))
            w_sc[rows, cols] = w
    act = _gelu(lax.dot_general(eu_ref[...], h_ref[...], (((1,), (1,)), ((), ())),
                                preferred_element_type=F32))
    p_sc[slot] = w_sc[...] * act.astype(BF16)


def _peer_dense(h2, eu, evt, a, n, b, r2, tb, ec):
    t, d = h2.shape
    ne = eu.shape[0] // ec
    big = pl.BlockSpec((PEER_HEADS, PEER_NKEYS, tb), lambda i, e: (0, 0, i))
    return pl.pallas_call(
        _peer_dense_kernel,
        grid=(t // tb, ne + 1),
        in_specs=[pl.BlockSpec((tb, d), lambda i, e: (i, 0)),
                  pl.BlockSpec((ec, d), lambda i, e: (jnp.minimum(e, ne - 1), 0)),
                  pl.BlockSpec((d, ec), lambda i, e: (0, jnp.maximum(e - 1, 0))),
                  big, big, big, big],
        out_specs=pl.BlockSpec((d, tb), lambda i, e: (0, i)),
        out_shape=jax.ShapeDtypeStruct((d, t), F32),
        scratch_shapes=[pltpu.VMEM((ec, tb), BF16), pltpu.VMEM((2, ec, tb), BF16)],
        compiler_params=_cparams(("parallel", "arbitrary")),
        name="peer_dense",
    )(h2, eu, evt, a, n, b, r2)


def _tail_b_kernel(x1_ref, ff_ref, p_ref, pg_ref, pp_ref, y_ref):
    x2 = x1_ref[...] + ff_ref[...]
    gate = jax.nn.sigmoid(jnp.dot(x2.astype(BF16), pg_ref[...], preferred_element_type=F32))
    y_ref[...] = x2 + gate * jnp.dot(p_ref[...].astype(BF16), pp_ref[...], preferred_element_type=F32)


def _tail_b(x1, ff, p, pg, pp):
    t, d = x1.shape
    tb = min(256, t)
    row = lambda n: pl.BlockSpec((tb, n), lambda i: (i, 0))
    full = lambda a: pl.BlockSpec(a.shape, lambda i: (0,) * a.ndim)
    return pl.pallas_call(
        _tail_b_kernel,
        grid=(t // tb,),
        in_specs=[row(d), row(d), row(p.shape[1]), full(pg), full(pp)],
        out_specs=row(d),
        out_shape=jax.ShapeDtypeStruct((t, d), F32),
        compiler_params=_cparams(("parallel",)),
        name="tail_b",
    )(x1, ff, p, pg, pp)


def _layer_tail(x, da, sa, gate, p, w):
    x1, h2 = _tail_a(x, da, sa, gate, w["wa"], w["wb"], w["wo"], w["n2"])
    a, n, b, r2 = _peer_route(h2, w["wq"], w["keys"])
    t = x.shape[0]
    tb = PEER_TB if t % PEER_TB == 0 else min(LANES, t)
    fft = _peer_dense(h2, w["eu"], w["evt"], a, n, b, r2, tb, PEER_EC)
    return _tail_b(x1, fft.T, p, w["pg"], w["pp"])


def _bias_tiles128(tab):
    assert MAX_DIST == LANES
    n = LANES
    tab = tab - tab[N_BUCKETS - 1:N_BUCKETS]
    f = jnp.concatenate([tab[_BUCKET], jnp.zeros((1, tab.shape[1]), F32)], axis=0)
    m = np.arange(2 * n)
    w = f[np.where(m == 0, 0, np.where(m < n, n - m, np.where(m == n, n, 2 * n - m)))].T
    skew = jnp.tile(w, (1, n))[:, :n * (2 * n - 1)].reshape(-1, n, 2 * n - 1)[:, :, :n]
    d = np.arange(n)[:, None] - np.arange(n)[None, :]
    keep = jnp.asarray(d >= 0)[None]
    return jnp.where(keep, skew, 0.0), jnp.where(keep, 0.0, skew)


def _block_bias(low, up, rows, cols, shift):
    zero = jnp.zeros_like(low)
    out = []
    for a in range(rows // LANES):
        row = []
        for b in range(cols // LANES):
            k = shift // LANES + a - b
            row.append(low if k == 0 else up if k == 1 else zero)
        out.append(jnp.concatenate(row, axis=2))
    return jnp.concatenate(out, axis=1)


def _decode_bias(tab, page, rep):
    tab = tab - tab[N_BUCKETS - 1:N_BUCKETS]
    last = jnp.repeat(tab[_bucket_of(page - np.arange(page))].T, rep, axis=0)
    new = jnp.repeat(jnp.broadcast_to(tab[0][:, None], (tab.shape[1], LANES)), rep, axis=0)
    return last, new


def _tile_lanes(g, reps):
    return jnp.tile(g.reshape(1, -1), (1, reps))


def _group_mean_matrix(width, group):
    idx = np.arange(width) // group
    return jnp.asarray((idx[:, None] == idx[None, :]).astype(np.float32) / group, BF16)


def kernel(x_prompt, x_sample, p_prompt, p_sample, cache_diff_k, cache_diff_v, cache_dsa_k, cache_dsa_v, cache_idx_k, page_table, rel_bias, norm1_g, w_in, diff_q_norm, diff_k_norm, diff_lambda_p, diff_subln, dsa_q_norm, dsa_k_norm, w_branch_a, w_branch_b, w_out, norm2_g, peer_wq, peer_keys, peer_u, peer_v, ple_gate, ple_proj):
    depth = w_in.shape[0]
    batch, seq, d_model = x_prompt.shape
    db, dec_seq, _ = x_sample.shape
    assert batch == 1 and dec_seq == 1
    n_pages = page_table.shape[1]
    page = cache_diff_k.shape[2]
    past = n_pages * page
    assert page == LANES and seq % DSA_CHUNK == 0
    k_sel_p = min(TOPK_MAX, seq // 4)
    k_sel_s = min(TOPK_MAX, (past + dec_seq) // 4)
    g_pages = max(g for g in (1, 2, 4, 8, DEC_PAGES) if n_pages % g == 0)

    tab_d = rel_bias[:, :DIFF_HEADS]
    tab_s = rel_bias[:, DIFF_HEADS:]
    low_d, up_d = _bias_tiles128(tab_d)
    bias_dd = _block_bias(low_d, up_d, DIFF_BLK, DIFF_BLK, 0)
    bias_dp = _block_bias(low_d, up_d, DIFF_BLK, DIFF_BLK, DIFF_BLK)
    low_s, up_s = _bias_tiles128(tab_s)
    bias3 = jnp.stack([_block_bias(low_s, up_s, DSA_ROWS, LANES, LANES) if k == 0 else
                       jnp.concatenate([jnp.zeros((DSA_HEADS, (k - 1) * LANES, LANES), F32),
                                        _block_bias(low_s, up_s, DSA_ROWS - (k - 1) * LANES, LANES, 0)],
                                       axis=1)
                       for k in range(DSA_ROWS // LANES + 1)], axis=1)
    dec_d_last, dec_d_new = _decode_bias(tab_d, page, 2)
    dec_s_last, dec_s_new = _decode_bias(tab_s, page, 1)
    g64 = _group_mean_matrix(DQ_W, DIFF_QK)
    g128 = _group_mean_matrix(B_WIDTH, DSA_HD)

    xp = x_prompt.reshape(seq, d_model)
    xs = x_sample.reshape(db, d_model)
    outs_p = {k: [] for k in ("dk", "dv", "sk", "sv", "ik")}
    outs_s = {k: [] for k in ("dk", "dv", "sk", "sv", "ik")}
    for l in range(depth):
        lam_init = 0.8 - 0.6 * math.exp(-0.3 * l)
        n_split = OFF_IKW + IKW_USED
        w_pad = jnp.concatenate(
            [w_in[l][:, :n_split], jnp.zeros((d_model, LANES - IKW_USED), F32), w_in[l][:, n_split:]],
            axis=1).astype(BF16)
        g1 = norm1_g[l].reshape(1, d_model)
        dqn = _tile_lanes(diff_q_norm[l], DQ_W // DIFF_QK)
        dkn = _tile_lanes(diff_k_norm[l], DQ_W // DIFF_QK)
        sqn = _tile_lanes(dsa_q_norm[l], DSA_HEADS)
        skn = dsa_k_norm[l].reshape(1, DSA_HD)
        lam_p = diff_lambda_p[l]
        subln = diff_subln[l].reshape(1, DIFF_V)
        tail_w = dict(
            wa=w_branch_a[l].astype(BF16), wb=w_branch_b[l].astype(BF16), wo=w_out[l].astype(BF16),
            n2=norm2_g[l].reshape(1, d_model), wq=peer_wq[l].astype(BF16),
            keys=peer_keys[l].astype(BF16), eu=peer_u[l].astype(BF16),
            evt=peer_v[l].T.astype(BF16), pg=ple_gate[l].astype(BF16), pp=ple_proj[l].astype(BF16))

        (qd, kd, kdb, vd, vdb, qs, ks, ksb, vs, vsb, qi, ikw, kib, gate) = _proj(
            xp, g1, w_pad, dqn, dkn, sqn, skn, g64, g128)
        q4 = qd.reshape(seq, DIFF_HEADS, 2, DIFF_QK).transpose(1, 2, 0, 3)
        kt4 = kdb.reshape(seq, DIFF_HEADS, 2, DIFF_QK).transpose(1, 2, 3, 0)
        v3 = vdb.reshape(seq, DIFF_HEADS, DIFF_V).transpose(1, 0, 2)
        da = _diff_attn(q4, kt4, v3, bias_dd, bias_dp, lam_p, subln, lam_init, DIFF_BLK)
        qi4 = qi.reshape(seq, IDX_HEADS, IDX_DIM).transpose(1, 0, 2)
        sa = _dsa_prompt(qi4, ikw, qs, kib[:, :IDX_DIM].T, ksb.T, vsb, bias3, k_sel_p,
                         DSA_ROWS, DSA_CHUNK)
        xp = _layer_tail(xp, da, sa, gate, p_prompt[l, 0], tail_w)
        outs_p["dk"].append(kd.reshape(batch, seq, DIFF_HEADS, 2, DIFF_QK))
        outs_p["dv"].append(vd.reshape(batch, seq, DIFF_HEADS, DIFF_V))
        outs_p["sk"].append(ks.reshape(batch, seq, DSA_HD))
        outs_p["sv"].append(vs.reshape(batch, seq, DSA_HD))
        outs_p["ik"].append(ikw[:, :IDX_DIM].reshape(batch, seq, IDX_DIM))

        (qd2, kd2, _, vd2, _, qs2, ks2, _, vs2, _, qi2, ikw2, _, gate2) = _proj(
            xs, g1, w_pad, dqn, dkn, sqn, skn, g64, g128)
        ki2 = ikw2[:, :IDX_DIM]
        wi2 = ikw2[:, IDX_DIM:IDX_DIM + IDX_HEADS]
        n_phys = cache_diff_k.shape[1]
        da_s, sc_past, sc_new = _sample_diff_idx(
            page_table,
            jnp.transpose(cache_diff_k[l], (0, 2, 3, 4, 1)).reshape(n_phys, DQ_W, page),
            cache_diff_v[l].reshape(n_phys, page * DIFF_HEADS, DIFF_V),
            jnp.transpose(cache_idx_k[l], (0, 2, 1)),
            qd2.reshape(db, 1, DQ_W), kd2.reshape(db, 1, DQ_W),
            jnp.repeat(vd2.reshape(db, DIFF_HEADS, DIFF_V), 2, axis=1),
            qi2.reshape(db, IDX_HEADS, IDX_DIM), qi2.astype(F32).reshape(db, IDX_HEADS, IDX_DIM),
            ki2.reshape(db, 1, IDX_DIM),
            jnp.broadcast_to(wi2[:, :, None], (db, IDX_HEADS, LANES)),
            dec_d_last, dec_d_new, lam_p, subln, lam_init, g_pages)
        scores = jnp.concatenate(
            [sc_past.reshape(db, past), sc_new.reshape(db, LANES)[:, :1],
             jnp.full((db, LANES - 1), -jnp.inf, F32)], axis=1)
        mask = _sample_select(scores, k_sel_s)
        sa_s = _sample_dsa(
            page_table, cache_dsa_k[l], cache_dsa_v[l],
            qs2.reshape(db, DSA_HEADS, DSA_HD), ks2.reshape(db, 1, DSA_HD), vs2.reshape(db, 1, DSA_HD),
            mask.reshape(db, 1, past + LANES), dec_s_last, dec_s_new, g_pages)
        xs = _layer_tail(xs, da_s.reshape(db, A_WIDTH), sa_s.reshape(db, B_WIDTH), gate2,
                         p_sample[l, :, 0], tail_w)
        outs_s["dk"].append(kd2.reshape(db, dec_seq, DIFF_HEADS, 2, DIFF_QK))
        outs_s["dv"].append(vd2.reshape(db, dec_seq, DIFF_HEADS, DIFF_V))
        outs_s["sk"].append(ks2.reshape(db, dec_seq, DSA_HD))
        outs_s["sv"].append(vs2.reshape(db, dec_seq, DSA_HD))
        outs_s["ik"].append(ki2.reshape(db, dec_seq, IDX_DIM))

    st = jnp.stack
    return (xp.reshape(batch, seq, d_model), xs.reshape(db, dec_seq, d_model),
            st(outs_p["dk"]), st(outs_p["dv"]), st(outs_p["sk"]), st(outs_p["sv"]), st(outs_p["ik"]),
            st(outs_s["dk"]), st(outs_s["dv"]), st(outs_s["sk"]), st(outs_s["sv"]), st(outs_s["ik"]))
```
